```python
import jax, jax.numpy as jnp
from jax import lax
import numpy as np

D_MODEL = 1024
BATCH = 16
SEQ = 2048
DEPTH = 1

MEM_LEN = 256
GLA_H = 4
GLA_QK = D_MODEL // 2
GLA_V = D_MODEL
GLA_DK = GLA_QK // GLA_H
GLA_DV = GLA_V // GLA_H
GLA_RANK = 16
GLA_TAU = 16.0
GLA_CHUNK = 64
SWA_HD = 64
SWA_HQ = D_MODEL // SWA_HD
SWA_HKV = 4
SWA_Q = SWA_HQ * SWA_HD
SWA_KV = SWA_HKV * SWA_HD
WINDOW = 128
ROPE_THETA = 10000.0
X_H = 4
X_HD = 256
X_W = X_H * X_HD
N_BRANCH = 3
IN_SIZES = (GLA_QK, GLA_QK, GLA_V, GLA_V, GLA_RANK,
            SWA_Q, SWA_KV, SWA_KV, SWA_Q,
            X_W, X_W,
            N_BRANCH * D_MODEL)
IN_COLS = sum(IN_SIZES)
EPS = 1e-6

kernel_name = "hybrid_gla_swa_sink_memxattn_gated"


def rmsnorm(x, g):
    xf = x.astype(jnp.float32)
    y = xf * lax.rsqrt(jnp.mean(xf * xf, axis=-1, keepdims=True) + EPS)
    return (y * g.astype(jnp.float32)).astype(x.dtype)


def rope(x, pos):
    half = x.shape[-1] // 2
    inv = ROPE_THETA ** (-jnp.arange(half, dtype=jnp.float32) / half)
    ang = pos.astype(jnp.float32)[:, None] * inv[None, :]
    cos = jnp.cos(ang)[None, :, None, :]
    sin = jnp.sin(ang)[None, :, None, :]
    xf = x.astype(jnp.float32)
    x1, x2 = xf[..., :half], xf[..., half:]
    out = jnp.concatenate([x1 * cos - x2 * sin, x2 * cos + x1 * sin], axis=-1)
    return out.astype(x.dtype)


def gla_chunked(q, k, v, log_a):
    B, S, H, DK = q.shape
    DV = v.shape[-1]
    C = GLA_CHUNK
    N = S // C

    def blk(t):
        return t.astype(jnp.float32).reshape(B, N, C, H, t.shape[-1]).transpose(0, 3, 1, 2, 4)

    qf = blk(q) * (DK ** -0.5)
    kf = blk(k)
    vf = blk(v)
    b = jnp.cumsum(blk(log_a), axis=3)
    b_last = b[:, :, :, -1:, :]
    q_dec = qf * jnp.exp(b)
    att = jnp.einsum('bhncd,bhnjd->bhncj', q_dec, kf * jnp.exp(-b))
    causal = jnp.tril(jnp.ones((C, C), dtype=bool))
    att = jnp.where(causal, att, 0.0)
    o_intra = jnp.einsum('bhncj,bhnjv->bhncv', att, vf)
    k_dec = kf * jnp.exp(b_last - b)
    chunk_decay = jnp.exp(b_last[:, :, :, 0, :])

    def step(state, inp):
        q_c, k_c, v_c, d_c = inp
        o_c = jnp.einsum('bhcd,bhdv->bhcv', q_c, state)
        state = d_c[..., None] * state + jnp.einsum('bhcd,bhcv->bhdv', k_c, v_c)
        return state, o_c

    xs = (jnp.moveaxis(q_dec, 2, 0), jnp.moveaxis(k_dec, 2, 0),
          jnp.moveaxis(vf, 2, 0), jnp.moveaxis(chunk_decay, 2, 0))
    _, o_inter = lax.scan(step, jnp.zeros((B, H, DK, DV), jnp.float32), xs)
    o = o_intra + jnp.moveaxis(o_inter, 0, 2)
    return o.transpose(0, 2, 3, 1, 4).reshape(B, S, H, DV)


def swa_sinks(q, k, v, sinks):
    B, S, HQ, D = q.shape
    HKV = k.shape[2]
    G = HQ // HKV
    W = WINDOW
    N = S // W
    qb = q.astype(jnp.float32).reshape(B, N, W, HKV, G, D)
    kb = k.astype(jnp.float32).reshape(B, N, W, HKV, D)
    vb = v.astype(jnp.float32).reshape(B, N, W, HKV, D)
    pad = ((0, 0), (1, 0), (0, 0), (0, 0), (0, 0))
    kk = jnp.concatenate([jnp.pad(kb, pad)[:, :N], kb], axis=2)
    vv = jnp.concatenate([jnp.pad(vb, pad)[:, :N], vb], axis=2)
    s = jnp.einsum('bnqhgd,bnkhd->bnhgqk', qb, kk) * (D ** -0.5)
    qi = jnp.arange(W)[:, None] + W
    kj = jnp.arange(2 * W)[None, :]
    band = (kj <= qi) & (kj > qi - W)
    has_prev = (jnp.arange(N) > 0)[:, None, None]
    valid = band[None] & (has_prev | (kj >= W)[None])
    s = jnp.where(valid[None, :, None, None], s, -jnp.inf)
    sink = sinks.astype(jnp.float32).reshape(1, 1, HKV, G, 1, 1)
    m = jnp.maximum(jnp.max(s, axis=-1, keepdims=True), sink)
    p = jnp.exp(s - m)
    p = p / (jnp.sum(p, axis=-1, keepdims=True) + jnp.exp(sink - m))
    o = jnp.einsum('bnhgqk,bnkhd->bnqhgd', p, vv)
    return o.reshape(B, S, HQ, D)


def setup_inputs(seed: int = 0) -> dict:
    key = jax.random.key(seed)
    ks = jax.random.split(key, 16)
    L = DEPTH

    def nrm(k, shape, scale):
        return jax.random.normal(k, shape, jnp.float32) * scale

    return {
        "x": nrm(ks[0], (BATCH, SEQ, D_MODEL), 1.0),
        "mem": nrm(ks[1], (BATCH, MEM_LEN, D_MODEL), 1.0),
        "g_mix": 1.0 + nrm(ks[2], (L, D_MODEL), 0.02),
        "g_mem": 1.0 + nrm(ks[3], (L, D_MODEL), 0.02),
        "w_in": nrm(ks[4], (L, D_MODEL, IN_COLS), D_MODEL ** -0.5),
        "b_in": nrm(ks[5], (L, IN_COLS), 0.01),
        "w_gla_gate_up": nrm(ks[6], (L, GLA_RANK, GLA_QK), GLA_RANK ** -0.5),
        "b_gla_gate": nrm(ks[7], (L, GLA_QK), 0.1),
        "g_gla_norm": 1.0 + nrm(ks[8], (L, GLA_DV), 0.02),
        "sinks": nrm(ks[9], (L, SWA_HQ), 0.5),
        "w_mem_kv": nrm(ks[10], (L, D_MODEL, 2 * X_W), D_MODEL ** -0.5),
        "w_br_gla": nrm(ks[11], (L, GLA_V, D_MODEL), GLA_V ** -0.5),
        "w_br_swa": nrm(ks[12], (L, SWA_Q, D_MODEL), SWA_Q ** -0.5),
        "w_br_mem": nrm(ks[13], (L, X_W, D_MODEL), X_W ** -0.5),
        "w_out": nrm(ks[14], (L, D_MODEL, D_MODEL), D_MODEL ** -0.5),
        "g_final": 1.0 + nrm(ks[15], (D_MODEL,), 0.02),
    }


def reference(x, mem, g_mix, g_mem, w_in, b_in, w_gla_gate_up, b_gla_gate, g_gla_norm, sinks,
              w_mem_kv, w_br_gla, w_br_swa, w_br_mem, w_out, g_final):
    B, S, _ = x.shape
    M = mem.shape[1]
    dt = x.dtype
    pos = jnp.arange(S)
    split_idx = tuple(int(i) for i in np.cumsum(IN_SIZES)[:-1])
    for l in range(DEPTH):
        h = rmsnorm(x, g_mix[l])
        proj = h @ w_in[l] + b_in[l]
        (gq, gk, gv, gz, glr, sq, sk, sv, sz, xq, xz, gates) = jnp.split(proj, split_idx, axis=-1)

        log_a = jax.nn.log_sigmoid((glr @ w_gla_gate_up[l] + b_gla_gate[l]).astype(jnp.float32)) / GLA_TAU
        o_a = gla_chunked(gq.reshape(B, S, GLA_H, GLA_DK), gk.reshape(B, S, GLA_H, GLA_DK),
                          gv.reshape(B, S, GLA_H, GLA_DV), log_a.reshape(B, S, GLA_H, GLA_DK))
        o_a = rmsnorm(o_a, g_gla_norm[l]).reshape(B, S, GLA_V).astype(dt)
        y_a = (o_a * jax.nn.silu(gz)) @ w_br_gla[l]

        q_b = rope(sq.reshape(B, S, SWA_HQ, SWA_HD), pos)
        k_b = rope(sk.reshape(B, S, SWA_HKV, SWA_HD), pos)
        o_b = swa_sinks(q_b, k_b, sv.reshape(B, S, SWA_HKV, SWA_HD), sinks[l])
        y_b = (o_b.reshape(B, S, SWA_Q).astype(dt) * jax.nn.silu(sz)) @ w_br_swa[l]

        mkv = rmsnorm(mem, g_mem[l]) @ w_mem_kv[l]
        mk = mkv[..., :X_W].reshape(B, M, X_H, X_HD).astype(jnp.float32)
        mv = mkv[..., X_W:].reshape(B, M, X_H, X_HD).astype(jnp.float32)
        s_c = jnp.einsum('bshd,bmhd->bhsm', xq.reshape(B, S, X_H, X_HD).astype(jnp.float32), mk) * (X_HD ** -0.5)
        p_c = jax.nn.softmax(s_c, axis=-1)
        o_c = jnp.einsum('bhsm,bmhd->bshd', p_c, mv).reshape(B, S, X_W).astype(dt)
        y_c = (o_c * jax.nn.silu(xz)) @ w_br_mem[l]

        g = jax.nn.sigmoid(gates.astype(jnp.float32)).reshape(B, S, N_BRANCH, D_MODEL)
        merged = (g[:, :, 0] * y_a.astype(jnp.float32) + g[:, :, 1] * y_b.astype(jnp.float32)
                  + g[:, :, 2] * y_c.astype(jnp.float32)).astype(dt)
        x = x + merged @ w_out[l]
    return rmsnorm(x, g_final)
```

```python
import functools

import numpy as np
import jax
import jax.numpy as jnp
from jax import lax
from jax.experimental import pallas as pl
from jax.experimental.pallas import tpu as pltpu

D_MODEL = 1024
MEM_LEN = 256
GLA_H = 4
GLA_QK = 512
GLA_V = 1024
GLA_DK = 128
GLA_DV = 256
GLA_RANK = 16
GLA_TAU = 16.0
GLA_CHUNK = 64
SWA_HD = 64
SWA_HALF = SWA_HD // 2
SWA_HQ = 16
SWA_HKV = 4
SWA_G = SWA_HQ // SWA_HKV
SWA_Q = 1024
SWA_KV = 256
WINDOW = 128
ROPE_THETA = 10000.0
X_H = 4
X_HD = 256
X_W = 1024
N_BRANCH = 3
EPS = 1e-6

LANES = 128

_IN_SIZES = (GLA_QK, GLA_QK, GLA_V, GLA_V, GLA_RANK, SWA_Q, SWA_KV, SWA_KV, SWA_Q, X_W, X_W,
             N_BRANCH * D_MODEL)
_IN_OFF = np.concatenate([[0], np.cumsum(_IN_SIZES)]).astype(np.int64)
(_R_GQ, _R_GK, _R_GV, _R_GZ, _R_GLR, _R_SQ, _R_SK, _R_SV, _R_SZ, _R_XQ, _R_XZ, _R_GATES) = (
    int(o) for o in _IN_OFF[:-1])

GLR_PAD = LANES
C_GQ = 0
C_GK = C_GQ + GLA_QK
C_GV = C_GK + GLA_QK
C_GZ = C_GV + GLA_V
C_GLR = C_GZ + GLA_V
C_SQ = C_GLR + GLR_PAD
C_SK = C_SQ + SWA_Q
C_SV = C_SK + SWA_KV
C_SZ = C_SV + SWA_KV
C_XQ = C_SZ + SWA_Q
C_XZ = C_XQ + X_W
C_GATES = C_XZ + X_W
C_TOTAL = C_GATES + N_BRANCH * D_MODEL


def _swa_q_perm():
    idx = np.empty(SWA_Q, np.int64)
    for t in range(2):
        for c in range(SWA_G):
            for j in range(SWA_HKV):
                for r in range(SWA_HALF):
                    idx[t * 512 + c * 128 + j * SWA_HALF + r] = (j * SWA_G + c) * SWA_HD + t * SWA_HALF + r
    return idx


def _swa_k_perm():
    idx = np.empty(SWA_KV, np.int64)
    for t in range(2):
        for j in range(SWA_HKV):
            for r in range(SWA_HALF):
                idx[t * 128 + j * SWA_HALF + r] = j * SWA_HD + t * SWA_HALF + r
    return idx


def _swa_o_perm():
    idx = np.empty(SWA_Q, np.int64)
    for c in range(SWA_G):
        for j in range(SWA_HKV):
            for d in range(SWA_HD):
                idx[c * 256 + j * SWA_HD + d] = (j * SWA_G + c) * SWA_HD + d
    return idx


def _in_col_index():
    parts = [
        _R_GQ + np.arange(GLA_QK), _R_GK + np.arange(GLA_QK), _R_GV + np.arange(GLA_V),
        _R_GZ + np.arange(GLA_V),
        np.concatenate([_R_GLR + np.arange(GLA_RANK), -np.ones(GLR_PAD - GLA_RANK, np.int64)]),
        _R_SQ + _swa_q_perm(), _R_SK + _swa_k_perm(), _R_SV + np.arange(SWA_KV),
        _R_SZ + _swa_o_perm(), _R_XQ + np.arange(X_W), _R_XZ + np.arange(X_W),
        _R_GATES + np.arange(N_BRANCH * D_MODEL),
    ]
    idx = np.concatenate(parts).astype(np.int64)
    assert idx.shape[0] == C_TOTAL
    return idx


def _dot(a, b):
    return jnp.dot(a, b, preferred_element_type=jnp.float32)


def _dot_nt(a, b):
    return lax.dot_general(a, b, (((1,), (1,)), ((), ())), preferred_element_type=jnp.float32)


def _dot_tn(a, b):
    return lax.dot_general(a, b, (((0,), (0,)), ((), ())), preferred_element_type=jnp.float32)


def _sigmoid(x):
    return 1.0 / (1.0 + jnp.exp(-x))


def _silu(x):
    return x * _sigmoid(x)


def _log_sigmoid(x):
    return jnp.minimum(x, 0.0) - jnp.log(1.0 + jnp.exp(-jnp.abs(x)))


def _split_bf16(x):
    hi = x.astype(jnp.bfloat16)
    lo = (x - hi.astype(jnp.float32)).astype(jnp.bfloat16)
    return hi, lo


def _mem_kv_kernel(mem_ref, g_ref, w_ref, out_ref):
    m = mem_ref[0]
    y = m * lax.rsqrt(jnp.mean(m * m, axis=-1, keepdims=True) + EPS) * g_ref[...]
    out_ref[0] = _dot(y.astype(jnp.bfloat16), w_ref[...]).astype(jnp.bfloat16)


def _mem_kv(mem, g_mem, w_mem_kv_bf16):
    B, M, D = mem.shape
    N = w_mem_kv_bf16.shape[1]
    return pl.pallas_call(
        _mem_kv_kernel,
        grid=(B,),
        in_specs=[
            pl.BlockSpec((1, M, D), lambda b: (b, 0, 0)),
            pl.BlockSpec((1, D), lambda b: (0, 0)),
            pl.BlockSpec((D, N), lambda b: (0, 0)),
        ],
        out_specs=pl.BlockSpec((1, M, N), lambda b: (b, 0, 0)),
        out_shape=jax.ShapeDtypeStruct((B, M, N), jnp.bfloat16),
        compiler_params=pltpu.CompilerParams(dimension_semantics=("arbitrary",)),
        name="mem_kv",
    )(mem, g_mem.reshape(1, D), w_mem_kv_bf16)


def _layer_kernel(sinks_ref, x_ref, mkv_ref, cos_ref, sin_ref, w_in_ref, b_in_ref, w_up_ref,
                  b_gate_ref, g_mix_ref, g_gla_ref, w_br_gla_ref, w_br_swa_ref, w_br_mem_ref,
                  w_out_ref, g_final_ref, tri_ref, blk_ref, bias_ref,
                  out_ref,
                  st_ref, kprev_ref, vprev_ref, oa_ref, ob_ref, oc_ref, p_ref,
                  *, seq_tile, apply_final_norm):
    T = seq_tile
    f32 = jnp.float32
    bf16 = jnp.bfloat16
    s_idx = pl.program_id(1)

    @pl.when(s_idx == 0)
    def _():
        st_ref[...] = jnp.zeros_like(st_ref)
        kprev_ref[...] = jnp.zeros_like(kprev_ref)
        vprev_ref[...] = jnp.zeros_like(vprev_ref)

    x = x_ref[0]
    h = x * lax.rsqrt(jnp.mean(x * x, axis=-1, keepdims=True) + EPS) * g_mix_ref[...]
    hb = h.astype(bf16)

    def proj(lo, width):
        return _dot(hb, w_in_ref[:, lo:lo + width]) + b_in_ref[:, lo:lo + width]

    def gate(i):
        return _sigmoid(proj(C_GATES + i * D_MODEL, D_MODEL))

    gq = proj(C_GQ, GLA_QK)
    gk = proj(C_GK, GLA_QK)
    gvb = proj(C_GV, GLA_V).astype(bf16)
    glr = proj(C_GLR, GLR_PAD)
    pre = _dot(glr.astype(bf16), w_up_ref[...]) + b_gate_ref[...]
    la = _log_sigmoid(pre) * (1.0 / GLA_TAU)
    la_hi, la_lo = _split_bf16(la)
    tri = tri_ref[...]
    blk = blk_ref[...]
    b = _dot(tri, la_hi) + _dot(tri, la_lo)
    bl = _dot(blk, la_hi) + _dot(blk, la_lo)
    qd = (gq * (GLA_DK ** -0.5) * jnp.exp(b)).astype(bf16)
    kn = (gk * jnp.exp(-b)).astype(bf16)
    kd = (gk * jnp.exp(bl - b)).astype(bf16)
    dec = jnp.exp(bl)
    ci = lax.broadcasted_iota(jnp.int32, (GLA_CHUNK, GLA_CHUNK), 0)
    cj = lax.broadcasted_iota(jnp.int32, (GLA_CHUNK, GLA_CHUNK), 1)
    causal = cj <= ci
    for c in range(T // GLA_CHUNK):
        r0 = c * GLA_CHUNK
        for hh in range(GLA_H):
            k0 = hh * GLA_DK
            v0 = hh * GLA_DV
            qd_c = qd[r0:r0 + GLA_CHUNK, k0:k0 + GLA_DK]
            v_c = gvb[r0:r0 + GLA_CHUNK, v0:v0 + GLA_DV]
            att = _dot_nt(qd_c, kn[r0:r0 + GLA_CHUNK, k0:k0 + GLA_DK])
            att = jnp.where(causal, att, 0.0).astype(bf16)
            st = st_ref[hh]
            oa_ref[r0:r0 + GLA_CHUNK, v0:v0 + GLA_DV] = _dot(att, v_c) + _dot_nt(qd_c, st.astype(bf16))
            st_ref[hh] = (dec[r0:r0 + 1, k0:k0 + GLA_DK] * st
                          + _dot_tn(v_c, kd[r0:r0 + GLA_CHUNK, k0:k0 + GLA_DK]))
    g_gla = g_gla_ref[...]
    gz = proj(C_GZ, GLA_V)
    oa_parts = []
    for hh in range(GLA_H):
        v0 = hh * GLA_DV
        seg = oa_ref[:, v0:v0 + GLA_DV]
        oa_parts.append(seg * lax.rsqrt(jnp.mean(seg * seg, axis=-1, keepdims=True) + EPS) * g_gla)
    oa = jnp.concatenate(oa_parts, axis=1)
    y_a = _dot((oa * _silu(gz)).astype(bf16), w_br_gla_ref[...])
    merged = gate(0) * y_a

    sq = proj(C_SQ, SWA_Q)
    sk = proj(C_SK, SWA_KV)
    vb = proj(C_SV, SWA_KV).astype(bf16)
    cos = cos_ref[...]
    sin = sin_ref[...]
    q1 = sq[:, :SWA_Q // 2]
    q2 = sq[:, SWA_Q // 2:]
    qscale = SWA_HD ** -0.5
    q1r = ((q1 * cos - q2 * sin) * qscale).astype(bf16)
    q2r = ((q2 * cos + q1 * sin) * qscale).astype(bf16)
    k1 = sk[:, :LANES]
    k2 = sk[:, LANES:]
    ck = cos[:, :LANES]
    sn = sin[:, :LANES]
    kr = jnp.concatenate([k1 * ck - k2 * sn, k2 * ck + k1 * sn], axis=1).astype(bf16)
    klane = lax.broadcasted_iota(jnp.int32, (1, SWA_KV), 1)
    kgrp = (klane % LANES) // SWA_HALF
    vgrp = klane // SWA_HD
    zero_b = jnp.zeros((), bf16)
    for n in range(T // WINDOW):
        r0 = n * WINDOW
        if n == 0:
            kp = kprev_ref[...]
            vp = vprev_ref[...]
            bias = bias_ref[jnp.where(s_idx == 0, 1, 0)]
        else:
            kp = kr[r0 - WINDOW:r0]
            vp = vb[r0 - WINDOW:r0]
            bias = bias_ref[0]
        kcat = jnp.concatenate([kp, kr[r0:r0 + WINDOW]], axis=0)
        vcat = jnp.concatenate([vp, vb[r0:r0 + WINDOW]], axis=0)
        rhs = jnp.concatenate([jnp.where(kgrp == j, kcat, zero_b) for j in range(SWA_HKV)], axis=0)
        vstk = jnp.concatenate([jnp.where(vgrp == j, vcat, zero_b) for j in range(SWA_HKV)], axis=0)
        lhs = jnp.concatenate(
            [jnp.concatenate([q1r[r0:r0 + WINDOW, c * LANES:(c + 1) * LANES],
                              q2r[r0:r0 + WINDOW, c * LANES:(c + 1) * LANES]], axis=1)
             for c in range(SWA_G)], axis=0)
        s = _dot_nt(lhs, rhs)
        for c in range(SWA_G):
            for j in range(SWA_HKV):
                sc = s[c * WINDOW:(c + 1) * WINDOW, j * 2 * WINDOW:(j + 1) * 2 * WINDOW] + bias
                sink = sinks_ref[j * SWA_G + c]
                m = jnp.maximum(jnp.max(sc, axis=-1, keepdims=True), sink)
                p = jnp.exp(sc - m)
                l = jnp.sum(p, axis=-1, keepdims=True) + jnp.exp(sink - m)
                p_ref[c * WINDOW:(c + 1) * WINDOW, j * 2 * WINDOW:(j + 1) * 2 * WINDOW] = (
                    p * (1.0 / l)).astype(bf16)
        o = _dot(p_ref[...], vstk)
        for c in range(SWA_G):
            ob_ref[r0:r0 + WINDOW, c * SWA_KV:(c + 1) * SWA_KV] = o[c * WINDOW:(c + 1) * WINDOW, :]
    kprev_ref[...] = kr[T - WINDOW:T]
    vprev_ref[...] = vb[T - WINDOW:T]
    sz = proj(C_SZ, SWA_Q)
    y_b = _dot((ob_ref[...] * _silu(sz)).astype(bf16), w_br_swa_ref[...])
    merged = merged + gate(1) * y_b

    xq = proj(C_XQ, X_W)
    for hh in range(X_H):
        c0 = hh * X_HD
        qh = (xq[:, c0:c0 + X_HD] * (X_HD ** -0.5)).astype(bf16)
        sc = _dot_nt(qh, mkv_ref[0, :, c0:c0 + X_HD])
        m = jnp.max(sc, axis=-1, keepdims=True)
        p = jnp.exp(sc - m)
        l = jnp.sum(p, axis=-1, keepdims=True)
        oc_ref[:, c0:c0 + X_HD] = _dot((p * (1.0 / l)).astype(bf16),
                                       mkv_ref[0, :, X_W + c0:X_W + c0 + X_HD])
    xz = proj(C_XZ, X_W)
    y_c = _dot((oc_ref[...] * _silu(xz)).astype(bf16), w_br_mem_ref[...])
    merged = merged + gate(2) * y_c

    xo = x + _dot(merged.astype(bf16), w_out_ref[...])
    if apply_final_norm:
        xo = xo * lax.rsqrt(jnp.mean(xo * xo, axis=-1, keepdims=True) + EPS) * g_final_ref[...]
    out_ref[0] = xo


def _seq_tile(S):
    for t in (256, 128):
        if S % t == 0:
            return t
    raise ValueError(f"sequence length {S} must be a multiple of {WINDOW}")


def _layer(x, mkv, cos_t, sin_t, w_in, b_in, w_up, b_gate, g_mix, g_gla, sinks, w_br_gla, w_br_swa,
           w_br_mem, w_out, g_final, apply_final_norm):
    B, S, D = x.shape
    T = _seq_tile(S)
    M = mkv.shape[1]
    n_tiles = S // T

    rows = np.arange(T)
    same_chunk = (rows[:, None] // GLA_CHUNK) == (rows[None, :] // GLA_CHUNK)
    tri = jnp.asarray(same_chunk & (rows[None, :] <= rows[:, None]), jnp.bfloat16)
    blk = jnp.asarray(same_chunk, jnp.bfloat16)
    qi = np.arange(WINDOW)[:, None]
    kj = np.arange(2 * WINDOW)[None, :]
    band = (kj <= qi + WINDOW) & (kj > qi)
    bias_np = np.stack([np.where(band, 0.0, -np.inf),
                        np.where(band & (kj >= WINDOW), 0.0, -np.inf)]).astype(np.float32)
    bias = jnp.asarray(bias_np)

    def const(shape):
        nd = len(shape)
        return pl.BlockSpec(shape, lambda b, s: (0,) * nd, pipeline_mode=pl.Buffered(1))

    in_specs = [
        pl.BlockSpec(memory_space=pltpu.SMEM),
        pl.BlockSpec((1, T, D), lambda b, s: (b, s, 0)),
        pl.BlockSpec((1, M, 2 * X_W), lambda b, s: (b, 0, 0)),
        pl.BlockSpec((T, SWA_Q // 2), lambda b, s: (s, 0)),
        pl.BlockSpec((T, SWA_Q // 2), lambda b, s: (s, 0)),
        const((D, C_TOTAL)),
        const((1, C_TOTAL)),
        const((GLR_PAD, GLA_QK)),
        const((1, GLA_QK)),
        const((1, D)),
        const((1, GLA_DV)),
        const((GLA_V, D)),
        const((SWA_Q, D)),
        const((X_W, D)),
        const((D, D)),
        const((1, D)),
        const((T, T)),
        const((T, T)),
        const((2, WINDOW, 2 * WINDOW)),
    ]
    scratch = [
        pltpu.VMEM((GLA_H, GLA_DV, GLA_DK), jnp.float32),
        pltpu.VMEM((WINDOW, SWA_KV), jnp.bfloat16),
        pltpu.VMEM((WINDOW, SWA_KV), jnp.bfloat16),
        pltpu.VMEM((T, GLA_V), jnp.float32),
        pltpu.VMEM((T, SWA_Q), jnp.float32),
        pltpu.VMEM((T, X_W), jnp.float32),
        pltpu.VMEM((SWA_G * WINDOW, SWA_HKV * 2 * WINDOW), jnp.bfloat16),
    ]
    kern = functools.partial(_layer_kernel, seq_tile=T, apply_final_norm=apply_final_norm)
    return pl.pallas_call(
        kern,
        grid=(B, n_tiles),
        in_specs=in_specs,
        out_specs=pl.BlockSpec((1, T, D), lambda b, s: (b, s, 0)),
        out_shape=jax.ShapeDtypeStruct((B, S, D), jnp.float32),
        scratch_shapes=scratch,
        compiler_params=pltpu.CompilerParams(
            dimension_semantics=("arbitrary", "arbitrary"),
            vmem_limit_bytes=56 * 1024 * 1024),
        name="layer",
    )(sinks, x, mkv, cos_t, sin_t, w_in, b_in, w_up, b_gate, g_mix, g_gla, w_br_gla, w_br_swa,
      w_br_mem, w_out, g_final, tri, blk, bias)


def kernel(x, mem, g_mix, g_mem, w_in, b_in, w_gla_gate_up, b_gla_gate, g_gla_norm, sinks, w_mem_kv,
           w_br_gla, w_br_swa, w_br_mem, w_out, g_final):
    B, S, D = x.shape
    depth = w_in.shape[0]
    bf16 = jnp.bfloat16

    col = _in_col_index()
    col_src = jnp.asarray(np.where(col >= 0, col, 0))
    col_keep = jnp.asarray(col >= 0)
    o_perm = jnp.asarray(_swa_o_perm())

    inv = ROPE_THETA ** (-jnp.arange(SWA_HALF, dtype=jnp.float32) / SWA_HALF)
    ang = jnp.arange(S).astype(jnp.float32)[:, None] * inv[None, :]
    reps = (SWA_Q // 2) // SWA_HALF
    cos_t = jnp.tile(jnp.cos(ang), (1, reps))
    sin_t = jnp.tile(jnp.sin(ang), (1, reps))

    for l in range(depth):
        w_in_k = jnp.where(col_keep[None, :], jnp.take(w_in[l], col_src, axis=1), 0.0).astype(bf16)
        b_in_k = jnp.where(col_keep, jnp.take(b_in[l], col_src), 0.0).reshape(1, C_TOTAL)
        w_up = jnp.zeros((GLR_PAD, GLA_QK), bf16).at[:GLA_RANK].set(w_gla_gate_up[l].astype(bf16))
        mkv = _mem_kv(mem, g_mem[l], w_mem_kv[l].astype(bf16))
        x = _layer(
            x, mkv, cos_t, sin_t, w_in_k, b_in_k, w_up, b_gla_gate[l].reshape(1, GLA_QK),
            g_mix[l].reshape(1, D), g_gla_norm[l].reshape(1, GLA_DV), sinks[l],
            w_br_gla[l].astype(bf16), jnp.take(w_br_swa[l], o_perm, axis=0).astype(bf16),
            w_br_mem[l].astype(bf16), w_out[l].astype(bf16), g_final.reshape(1, D),
            apply_final_norm=(l == depth - 1))
    return x
```

```python
import functools

import numpy as np
import jax
import jax.numpy as jnp
from jax import lax
from jax.experimental import pallas as pl
from jax.experimental.pallas import tpu as pltpu

D_MODEL = 1024
MEM_LEN = 256
GLA_H = 4
GLA_QK = 512
GLA_V = 1024
GLA_DK = 128
GLA_DV = 256
GLA_RANK = 16
GLA_TAU = 16.0
GLA_CHUNK = 64
SWA_HD = 64
SWA_HALF = SWA_HD // 2
SWA_HQ = 16
SWA_HKV = 4
SWA_G = SWA_HQ // SWA_HKV
SWA_Q = 1024
SWA_KV = 256
WINDOW = 128
ROPE_THETA = 10000.0
X_H = 4
X_HD = 256
X_W = 1024
N_BRANCH = 3
EPS = 1e-6

LANES = 128

_IN_SIZES = (GLA_QK, GLA_QK, GLA_V, GLA_V, GLA_RANK, SWA_Q, SWA_KV, SWA_KV, SWA_Q, X_W, X_W,
             N_BRANCH * D_MODEL)
_IN_OFF = np.concatenate([[0], np.cumsum(_IN_SIZES)]).astype(np.int64)
(_R_GQ, _R_GK, _R_GV, _R_GZ, _R_GLR, _R_SQ, _R_SK, _R_SV, _R_SZ, _R_XQ, _R_XZ, _R_GATES) = (
    int(o) for o in _IN_OFF[:-1])

GLR_PAD = LANES
C_GQ = 0
C_GK = C_GQ + GLA_QK
C_GV = C_GK + GLA_QK
C_GZ = C_GV + GLA_V
C_GLR = C_GZ + GLA_V
C_SQ = C_GLR + GLR_PAD
C_SK = C_SQ + SWA_Q
C_SV = C_SK + SWA_KV
C_SZ = C_SV + SWA_KV
C_XQ = C_SZ + SWA_Q
C_XZ = C_XQ + X_W
C_GATES = C_XZ + X_W
C_TOTAL = C_GATES + N_BRANCH * D_MODEL


def _swa_q_perm():
    idx = np.empty(SWA_Q, np.int64)
    for t in range(2):
        for c in range(SWA_G):
            for j in range(SWA_HKV):
                for r in range(SWA_HALF):
                    idx[t * 512 + c * 128 + j * SWA_HALF + r] = (j * SWA_G + c) * SWA_HD + t * SWA_HALF + r
    return idx


def _swa_k_perm():
    idx = np.empty(SWA_KV, np.int64)
    for t in range(2):
        for j in range(SWA_HKV):
            for r in range(SWA_HALF):
                idx[t * 128 + j * SWA_HALF + r] = j * SWA_HD + t * SWA_HALF + r
    return idx


def _swa_o_perm():
    idx = np.empty(SWA_Q, np.int64)
    for c in range(SWA_G):
        for j in range(SWA_HKV):
            for d in range(SWA_HD):
                idx[c * 256 + j * SWA_HD + d] = (j * SWA_G + c) * SWA_HD + d
    return idx


def _in_col_index():
    parts = [
        _R_GQ + np.arange(GLA_QK), _R_GK + np.arange(GLA_QK), _R_GV + np.arange(GLA_V),
        _R_GZ + np.arange(GLA_V),
        np.concatenate([_R_GLR + np.arange(GLA_RANK), -np.ones(GLR_PAD - GLA_RANK, np.int64)]),
        _R_SQ + _swa_q_perm(), _R_SK + _swa_k_perm(), _R_SV + np.arange(SWA_KV),
        _R_SZ + _swa_o_perm(), _R_XQ + np.arange(X_W), _R_XZ + np.arange(X_W),
        _R_GATES + np.arange(N_BRANCH * D_MODEL),
    ]
    idx = np.concatenate(parts).astype(np.int64)
    assert idx.shape[0] == C_TOTAL
    return idx


def _dot(a, b):
    return jnp.dot(a, b, preferred_element_type=jnp.float32)


def _dot_nt(a, b):
    return lax.dot_general(a, b, (((1,), (1,)), ((), ())), preferred_element_type=jnp.float32)


def _dot_tn(a, b):
    return lax.dot_general(a, b, (((0,), (0,)), ((), ())), preferred_element_type=jnp.float32)


def _sigmoid(x):
    return 1.0 / (1.0 + jnp.exp(-x))


def _silu(x):
    return x * _sigmoid(x)


def _log_sigmoid(x):
    return jnp.minimum(x, 0.0) - jnp.log(1.0 + jnp.exp(-jnp.abs(x)))


def _split_bf16(x):
    hi = x.astype(jnp.bfloat16)
    lo = (x - hi.astype(jnp.float32)).astype(jnp.bfloat16)
    return hi, lo


def _mem_kv_kernel(mem_ref, g_ref, w_ref, out_ref):
    m = mem_ref[0]
    y = m * lax.rsqrt(jnp.mean(m * m, axis=-1, keepdims=True) + EPS) * g_ref[...]
    out_ref[0] = _dot(y.astype(jnp.bfloat16), w_ref[...]).astype(jnp.bfloat16)


def _mem_kv(mem, g_mem, w_mem_kv_bf16):
    B, M, D = mem.shape
    N = w_mem_kv_bf16.shape[1]
    return pl.pallas_call(
        _mem_kv_kernel,
        grid=(B,),
        in_specs=[
            pl.BlockSpec((1, M, D), lambda b: (b, 0, 0)),
            pl.BlockSpec((1, D), lambda b: (0, 0)),
            pl.BlockSpec((D, N), lambda b: (0, 0)),
        ],
        out_specs=pl.BlockSpec((1, M, N), lambda b: (b, 0, 0)),
        out_shape=jax.ShapeDtypeStruct((B, M, N), jnp.bfloat16),
        compiler_params=pltpu.CompilerParams(dimension_semantics=("arbitrary",)),
        name="mem_kv",
    )(mem, g_mem.reshape(1, D), w_mem_kv_bf16)


def _layer_kernel(sinks_ref, x_ref, mkv_ref, cos_ref, sin_ref, w_in_ref, b_in_ref, w_up_ref,
                  b_gate_ref, g_mix_ref, g_gla_ref, w_br_gla_ref, w_br_swa_ref, w_br_mem_ref,
                  w_out_ref, g_final_ref, tri_ref, blk_ref, bias_ref,
                  out_ref,
                  st_ref, kprev_ref, vprev_ref, oa_ref, ob_ref, oc_ref, p_ref,
                  *, seq_tile, apply_final_norm):
    T = seq_tile
    f32 = jnp.float32
    bf16 = jnp.bfloat16
    s_idx = pl.program_id(1)

    rd = s_idx % 2
    wr = 1 - rd

    @pl.when(s_idx == 0)
    def _():
        st_ref[0] = jnp.zeros(st_ref.shape[1:], st_ref.dtype)
        kprev_ref[0] = jnp.zeros(kprev_ref.shape[1:], kprev_ref.dtype)
        vprev_ref[0] = jnp.zeros(vprev_ref.shape[1:], vprev_ref.dtype)

    x = x_ref[0]
    h = x * lax.rsqrt(jnp.mean(x * x, axis=-1, keepdims=True) + EPS) * g_mix_ref[...]
    hb = h.astype(bf16)

    def proj(lo, width):
        return _dot(hb, w_in_ref[:, lo:lo + width]) + b_in_ref[:, lo:lo + width]

    def gate(i):
        return _sigmoid(proj(C_GATES + i * D_MODEL, D_MODEL))

    gq = proj(C_GQ, GLA_QK)
    gk = proj(C_GK, GLA_QK)
    gvb = proj(C_GV, GLA_V).astype(bf16)
    glr = proj(C_GLR, GLR_PAD)
    pre = _dot(glr.astype(bf16), w_up_ref[...]) + b_gate_ref[...]
    la = _log_sigmoid(pre) * (1.0 / GLA_TAU)
    la_hi, la_lo = _split_bf16(la)
    tri = tri_ref[...]
    blk = blk_ref[...]
    b = _dot(tri, la_hi) + _dot(tri, la_lo)
    bl = _dot(blk, la_hi) + _dot(blk, la_lo)
    qd = (gq * (GLA_DK ** -0.5) * jnp.exp(b)).astype(bf16)
    kn = (gk * jnp.exp(-b)).astype(bf16)
    kd = (gk * jnp.exp(bl - b)).astype(bf16)
    dec = jnp.exp(bl)
    ci = lax.broadcasted_iota(jnp.int32, (GLA_CHUNK, GLA_CHUNK), 0)
    cj = lax.broadcasted_iota(jnp.int32, (GLA_CHUNK, GLA_CHUNK), 1)
    causal = cj <= ci
    state = [st_ref[rd, hh] for hh in range(GLA_H)]
    for c in range(T // GLA_CHUNK):
        r0 = c * GLA_CHUNK
        for hh in range(GLA_H):
            k0 = hh * GLA_DK
            v0 = hh * GLA_DV
            qd_c = qd[r0:r0 + GLA_CHUNK, k0:k0 + GLA_DK]
            v_c = gvb[r0:r0 + GLA_CHUNK, v0:v0 + GLA_DV]
            att = _dot_nt(qd_c, kn[r0:r0 + GLA_CHUNK, k0:k0 + GLA_DK])
            att = jnp.where(causal, att, 0.0).astype(bf16)
            st = state[hh]
            oa_ref[r0:r0 + GLA_CHUNK, v0:v0 + GLA_DV] = _dot(att, v_c) + _dot_nt(qd_c, st.astype(bf16))
            state[hh] = (dec[r0:r0 + 1, k0:k0 + GLA_DK] * st
                         + _dot_tn(v_c, kd[r0:r0 + GLA_CHUNK, k0:k0 + GLA_DK]))
    for hh in range(GLA_H):
        st_ref[wr, hh] = state[hh]
    g_gla = g_gla_ref[...]
    gz = proj(C_GZ, GLA_V)
    oa_parts = []
    for hh in range(GLA_H):
        v0 = hh * GLA_DV
        seg = oa_ref[:, v0:v0 + GLA_DV]
        oa_parts.append(seg * lax.rsqrt(jnp.mean(seg * seg, axis=-1, keepdims=True) + EPS) * g_gla)
    oa = jnp.concatenate(oa_parts, axis=1)
    y_a = _dot((oa * _silu(gz)).astype(bf16), w_br_gla_ref[...])
    merged = gate(0) * y_a

    sq = proj(C_SQ, SWA_Q)
    sk = proj(C_SK, SWA_KV)
    vb = proj(C_SV, SWA_KV).astype(bf16)
    cos = cos_ref[...]
    sin = sin_ref[...]
    q1 = sq[:, :SWA_Q // 2]
    q2 = sq[:, SWA_Q // 2:]
    qscale = SWA_HD ** -0.5
    q1r = ((q1 * cos - q2 * sin) * qscale).astype(bf16)
    q2r = ((q2 * cos + q1 * sin) * qscale).astype(bf16)
    k1 = sk[:, :LANES]
    k2 = sk[:, LANES:]
    ck = cos[:, :LANES]
    sn = sin[:, :LANES]
    kr = jnp.concatenate([k1 * ck - k2 * sn, k2 * ck + k1 * sn], axis=1).astype(bf16)
    klane = lax.broadcasted_iota(jnp.int32, (1, SWA_KV), 1)
    kgrp = (klane % LANES) // SWA_HALF
    vgrp = klane // SWA_HD
    zero_b = jnp.zeros((), bf16)
    for n in range(T // WINDOW):
        r0 = n * WINDOW
        if n == 0:
            kp = kprev_ref[rd]
            vp = vprev_ref[rd]
            bias = bias_ref[jnp.where(s_idx == 0, 1, 0)]
        else:
            kp = kr[r0 - WINDOW:r0]
            vp = vb[r0 - WINDOW:r0]
            bias = bias_ref[0]
        kcat = jnp.concatenate([kp, kr[r0:r0 + WINDOW]], axis=0)
        vcat = jnp.concatenate([vp, vb[r0:r0 + WINDOW]], axis=0)
        rhs = jnp.concatenate([jnp.where(kgrp == j, kcat, zero_b) for j in range(SWA_HKV)], axis=0)
        vstk = jnp.concatenate([jnp.where(vgrp == j, vcat, zero_b) for j in range(SWA_HKV)], axis=0)
        lhs = jnp.concatenate(
            [jnp.concatenate([q1r[r0:r0 + WINDOW, c * LANES:(c + 1) * LANES],
                              q2r[r0:r0 + WINDOW, c * LANES:(c + 1) * LANES]], axis=1)
             for c in range(SWA_G)], axis=0)
        s = _dot_nt(lhs, rhs)
        for c in range(SWA_G):
            for j in range(SWA_HKV):
                sc = s[c * WINDOW:(c + 1) * WINDOW, j * 2 * WINDOW:(j + 1) * 2 * WINDOW] + bias
                sink = sinks_ref[j * SWA_G + c]
                m = jnp.maximum(jnp.max(sc, axis=-1, keepdims=True), sink)
                p = jnp.exp(sc - m)
                l = jnp.sum(p, axis=-1, keepdims=True) + jnp.exp(sink - m)
                p_ref[n, c * WINDOW:(c + 1) * WINDOW, j * 2 * WINDOW:(j + 1) * 2 * WINDOW] = (
                    p * (1.0 / l)).astype(bf16)
        o = _dot(p_ref[n], vstk)
        for c in range(SWA_G):
            ob_ref[r0:r0 + WINDOW, c * SWA_KV:(c + 1) * SWA_KV] = o[c * WINDOW:(c + 1) * WINDOW, :]
    kprev_ref[wr] = kr[T - WINDOW:T]
    vprev_ref[wr] = vb[T - WINDOW:T]
    sz = proj(C_SZ, SWA_Q)
    y_b = _dot((ob_ref[...] * _silu(sz)).astype(bf16), w_br_swa_ref[...])
    merged = merged + gate(1) * y_b

    xq = proj(C_XQ, X_W)
    for hh in range(X_H):
        c0 = hh * X_HD
        qh = (xq[:, c0:c0 + X_HD] * (X_HD ** -0.5)).astype(bf16)
        sc = _dot_nt(qh, mkv_ref[0, :, c0:c0 + X_HD])
        m = jnp.max(sc, axis=-1, keepdims=True)
        p = jnp.exp(sc - m)
        l = jnp.sum(p, axis=-1, keepdims=True)
        oc_ref[:, c0:c0 + X_HD] = _dot((p * (1.0 / l)).astype(bf16),
                                       mkv_ref[0, :, X_W + c0:X_W + c0 + X_HD])
    xz = proj(C_XZ, X_W)
    y_c = _dot((oc_ref[...] * _silu(xz)).astype(bf16), w_br_mem_ref[...])
    merged = merged + gate(2) * y_c

    xo = x + _dot(merged.astype(bf16), w_out_ref[...])
    if apply_final_norm:
        xo = xo * lax.rsqrt(jnp.mean(xo * xo, axis=-1, keepdims=True) + EPS) * g_final_ref[...]
    out_ref[0] = xo


def _seq_tile(S):
    for t in (256, 128):
        if S % t == 0:
            return t
    raise ValueError(f"sequence length {S} must be a multiple of {WINDOW}")


def _layer(x, mkv, cos_t, sin_t, w_in, b_in, w_up, b_gate, g_mix, g_gla, sinks, w_br_gla, w_br_swa,
           w_br_mem, w_out, g_final, apply_final_norm):
    B, S, D = x.shape
    T = _seq_tile(S)
    M = mkv.shape[1]
    n_tiles = S // T

    rows = np.arange(T)
    same_chunk = (rows[:, None] // GLA_CHUNK) == (rows[None, :] // GLA_CHUNK)
    tri = jnp.asarray(same_chunk & (rows[None, :] <= rows[:, None]), jnp.bfloat16)
    blk = jnp.asarray(same_chunk, jnp.bfloat16)
    qi = np.arange(WINDOW)[:, None]
    kj = np.arange(2 * WINDOW)[None, :]
    band = (kj <= qi + WINDOW) & (kj > qi)
    bias_np = np.stack([np.where(band, 0.0, -np.inf),
                        np.where(band & (kj >= WINDOW), 0.0, -np.inf)]).astype(np.float32)
    bias = jnp.asarray(bias_np)

    def const(shape):
        nd = len(shape)
        return pl.BlockSpec(shape, lambda b, s: (0,) * nd, pipeline_mode=pl.Buffered(1))

    in_specs = [
        pl.BlockSpec(memory_space=pltpu.SMEM),
        pl.BlockSpec((1, T, D), lambda b, s: (b, s, 0)),
        pl.BlockSpec((1, M, 2 * X_W), lambda b, s: (b, 0, 0)),
        pl.BlockSpec((T, SWA_Q // 2), lambda b, s: (s, 0)),
        pl.BlockSpec((T, SWA_Q // 2), lambda b, s: (s, 0)),
        const((D, C_TOTAL)),
        const((1, C_TOTAL)),
        const((GLR_PAD, GLA_QK)),
        const((1, GLA_QK)),
        const((1, D)),
        const((1, GLA_DV)),
        const((GLA_V, D)),
        const((SWA_Q, D)),
        const((X_W, D)),
        const((D, D)),
        const((1, D)),
        const((T, T)),
        const((T, T)),
        const((2, WINDOW, 2 * WINDOW)),
    ]
    scratch = [
        pltpu.VMEM((2, GLA_H, GLA_DV, GLA_DK), jnp.float32),
        pltpu.VMEM((2, WINDOW, SWA_KV), jnp.bfloat16),
        pltpu.VMEM((2, WINDOW, SWA_KV), jnp.bfloat16),
        pltpu.VMEM((T, GLA_V), jnp.float32),
        pltpu.VMEM((T, SWA_Q), jnp.float32),
        pltpu.VMEM((T, X_W), jnp.float32),
        pltpu.VMEM((T // WINDOW, SWA_G * WINDOW, SWA_HKV * 2 * WINDOW), jnp.bfloat16),
    ]
    kern = functools.partial(_layer_kernel, seq_tile=T, apply_final_norm=apply_final_norm)
    return pl.pallas_call(
        kern,
        grid=(B, n_tiles),
        in_specs=in_specs,
        out_specs=pl.BlockSpec((1, T, D), lambda b, s: (b, s, 0)),
        out_shape=jax.ShapeDtypeStruct((B, S, D), jnp.float32),
        scratch_shapes=scratch,
        compiler_params=pltpu.CompilerParams(
            dimension_semantics=("arbitrary", "arbitrary"),
            vmem_limit_bytes=56 * 1024 * 1024),
        name="layer",
    )(sinks, x, mkv, cos_t, sin_t, w_in, b_in, w_up, b_gate, g_mix, g_gla, w_br_gla, w_br_swa,
      w_br_mem, w_out, g_final, tri, blk, bias)


def kernel(x, mem, g_mix, g_mem, w_in, b_in, w_gla_gate_up, b_gla_gate, g_gla_norm, sinks, w_mem_kv,
           w_br_gla, w_br_swa, w_br_mem, w_out, g_final):
    B, S, D = x.shape
    depth = w_in.shape[0]
    bf16 = jnp.bfloat16

    col = _in_col_index()
    col_src = jnp.asarray(np.where(col >= 0, col, 0))
    col_keep = jnp.asarray(col >= 0)
    o_perm = jnp.asarray(_swa_o_perm())

    inv = ROPE_THETA ** (-jnp.arange(SWA_HALF, dtype=jnp.float32) / SWA_HALF)
    ang = jnp.arange(S).astype(jnp.float32)[:, None] * inv[None, :]
    reps = (SWA_Q // 2) // SWA_HALF
    cos_t = jnp.tile(jnp.cos(ang), (1, reps))
    sin_t = jnp.tile(jnp.sin(ang), (1, reps))

    for l in range(depth):
        w_in_k = jnp.where(col_keep[None, :], jnp.take(w_in[l], col_src, axis=1), 0.0).astype(bf16)
        b_in_k = jnp.where(col_keep, jnp.take(b_in[l], col_src), 0.0).reshape(1, C_TOTAL)
        w_up = jnp.zeros((GLR_PAD, GLA_QK), bf16).at[:GLA_RANK].set(w_gla_gate_up[l].astype(bf16))
        mkv = _mem_kv(mem, g_mem[l], w_mem_kv[l].astype(bf16))
        x = _layer(
            x, mkv, cos_t, sin_t, w_in_k, b_in_k, w_up, b_gla_gate[l].reshape(1, GLA_QK),
            g_mix[l].reshape(1, D), g_gla_norm[l].reshape(1, GLA_DV), sinks[l],
            w_br_gla[l].astype(bf16), jnp.take(w_br_swa[l], o_perm, axis=0).astype(bf16),
            w_br_mem[l].astype(bf16), w_out[l].astype(bf16), g_final.reshape(1, D),
            apply_final_norm=(l == depth - 1))
    return x
```

```python
import functools

import numpy as np
import jax
import jax.numpy as jnp
from jax import lax
from jax.experimental import pallas as pl
from jax.experimental.pallas import tpu as pltpu

D_MODEL = 1024
MEM_LEN = 256
GLA_H = 4
GLA_QK = 512
GLA_V = 1024
GLA_DK = 128
GLA_DV = 256
GLA_RANK = 16
GLA_TAU = 16.0
GLA_CHUNK = 64
GLA_TILE = 256
SWA_HD = 64
SWA_HALF = SWA_HD // 2
SWA_HQ = 16
SWA_HKV = 4
SWA_G = SWA_HQ // SWA_HKV
SWA_Q = 1024
SWA_KV = 256
WINDOW = 128
ROPE_THETA = 10000.0
X_H = 4
X_HD = 256
X_W = 1024
N_BRANCH = 3
EPS = 1e-6

LANES = 128

_IN_SIZES = (GLA_QK, GLA_QK, GLA_V, GLA_V, GLA_RANK, SWA_Q, SWA_KV, SWA_KV, SWA_Q, X_W, X_W,
             N_BRANCH * D_MODEL)
_IN_OFF = np.concatenate([[0], np.cumsum(_IN_SIZES)]).astype(np.int64)
(_R_GQ, _R_GK, _R_GV, _R_GZ, _R_GLR, _R_SQ, _R_SK, _R_SV, _R_SZ, _R_XQ, _R_XZ, _R_GATES) = (
    int(o) for o in _IN_OFF[:-1])

GLR_PAD = LANES
C_GQ = 0
C_GK = C_GQ + GLA_QK
C_GV = C_GK + GLA_QK
C_GZ = C_GV + GLA_V
C_GLR = C_GZ + GLA_V
C_SQ = C_GLR + GLR_PAD
C_SK = C_SQ + SWA_Q
C_SV = C_SK + SWA_KV
C_SZ = C_SV + SWA_KV
C_XQ = C_SZ + SWA_Q
C_XZ = C_XQ + X_W
C_GATES = C_XZ + X_W
C_TOTAL = C_GATES + N_BRANCH * D_MODEL


def _swa_q_cols(a):
    lead = a.shape[:-1]
    nl = len(lead)
    a = a.reshape(lead + (SWA_HKV, SWA_G, 2, SWA_HALF))
    a = jnp.transpose(a, tuple(range(nl)) + (nl + 2, nl + 1, nl, nl + 3))
    return a.reshape(lead + (SWA_Q,))


def _swa_k_cols(a):
    lead = a.shape[:-1]
    nl = len(lead)
    a = a.reshape(lead + (SWA_HKV, 2, SWA_HALF))
    a = jnp.transpose(a, tuple(range(nl)) + (nl + 1, nl, nl + 2))
    return a.reshape(lead + (SWA_KV,))


def _swa_o_cols(a):
    lead = a.shape[:-1]
    nl = len(lead)
    a = a.reshape(lead + (SWA_HKV, SWA_G, SWA_HD))
    a = jnp.transpose(a, tuple(range(nl)) + (nl + 1, nl, nl + 2))
    return a.reshape(lead + (SWA_Q,))


def _kernel_columns(a):
    lead = a.shape[:-1]
    seg = lambda off, n: lax.slice_in_dim(a, off, off + n, axis=a.ndim - 1)
    out = jnp.concatenate([
        seg(_R_GQ, _R_GLR + GLA_RANK - _R_GQ),
        jnp.zeros(lead + (GLR_PAD - GLA_RANK,), a.dtype),
        _swa_q_cols(seg(_R_SQ, SWA_Q)),
        _swa_k_cols(seg(_R_SK, SWA_KV)),
        seg(_R_SV, SWA_KV),
        _swa_o_cols(seg(_R_SZ, SWA_Q)),
        seg(_R_XQ, int(_IN_OFF[-1]) - _R_XQ),
    ], axis=-1)
    assert out.shape[-1] == C_TOTAL
    return out


def _dot(a, b):
    return jnp.dot(a, b, preferred_element_type=jnp.float32)


def _dot_nt(a, b):
    return lax.dot_general(a, b, (((1,), (1,)), ((), ())), preferred_element_type=jnp.float32)


def _dot_tn(a, b):
    return lax.dot_general(a, b, (((0,), (0,)), ((), ())), preferred_element_type=jnp.float32)


def _sigmoid(x):
    return 1.0 / (1.0 + jnp.exp(-x))


def _silu(x):
    return x * _sigmoid(x)


def _log_sigmoid(x):
    return jnp.minimum(x, 0.0) - jnp.log(1.0 + jnp.exp(-jnp.abs(x)))


def _split_bf16(x):
    hi = x.astype(jnp.bfloat16)
    lo = (x - hi.astype(jnp.float32)).astype(jnp.bfloat16)
    return hi, lo


def _mem_kv_kernel(mem_ref, g_ref, w_ref, out_ref):
    m = mem_ref[0]
    y = m * lax.rsqrt(jnp.mean(m * m, axis=-1, keepdims=True) + EPS) * g_ref[...]
    out_ref[0] = _dot(y.astype(jnp.bfloat16), w_ref[...]).astype(jnp.bfloat16)


def _mem_kv(mem, g_mem, w_mem_kv_bf16):
    B, M, D = mem.shape
    N = w_mem_kv_bf16.shape[1]
    return pl.pallas_call(
        _mem_kv_kernel,
        grid=(B,),
        in_specs=[
            pl.BlockSpec((1, M, D), lambda b: (b, 0, 0)),
            pl.BlockSpec((1, D), lambda b: (0, 0)),
            pl.BlockSpec((D, N), lambda b: (0, 0)),
        ],
        out_specs=pl.BlockSpec((1, M, N), lambda b: (b, 0, 0)),
        out_shape=jax.ShapeDtypeStruct((B, M, N), jnp.bfloat16),
        compiler_params=pltpu.CompilerParams(dimension_semantics=("arbitrary",)),
        name="mem_kv",
    )(mem, g_mem.reshape(1, D), w_mem_kv_bf16)


def _layer_kernel(sinks_ref, x_ref, mkv_ref, cos_ref, sin_ref, w_in_ref, b_in_ref, w_up_ref,
                  b_gate_ref, g_mix_ref, g_gla_ref, w_br_gla_ref, w_br_swa_ref, w_br_mem_ref,
                  w_out_ref, g_final_ref, tri_ref, blk_ref, bias_ref,
                  out_ref,
                  st_ref, kprev_ref, vprev_ref, ya_ref, ob_ref, oc_ref, p_ref,
                  *, seq_tile, apply_final_norm):
    T = seq_tile
    f32 = jnp.float32
    bf16 = jnp.bfloat16
    s_idx = pl.program_id(1)

    rd = s_idx % 2
    wr = 1 - rd

    @pl.when(s_idx == 0)
    def _():
        st_ref[0] = jnp.zeros(st_ref.shape[1:], st_ref.dtype)
        kprev_ref[0] = jnp.zeros(kprev_ref.shape[1:], kprev_ref.dtype)
        vprev_ref[0] = jnp.zeros(vprev_ref.shape[1:], vprev_ref.dtype)

    x = x_ref[0]
    h = x * lax.rsqrt(jnp.mean(x * x, axis=-1, keepdims=True) + EPS) * g_mix_ref[...]
    hb = h.astype(bf16)

    def proj(lo, width):
        return _dot(hb, w_in_ref[:, lo:lo + width]) + b_in_ref[:, lo:lo + width]

    def gate(i):
        return _sigmoid(proj(C_GATES + i * D_MODEL, D_MODEL))

    gq = proj(C_GQ, GLA_QK)
    gk = proj(C_GK, GLA_QK)
    gvb = proj(C_GV, GLA_V).astype(bf16)
    gz = proj(C_GZ, GLA_V)
    glr = proj(C_GLR, GLR_PAD)
    pre = _dot(glr.astype(bf16), w_up_ref[...]) + b_gate_ref[...]
    la = _log_sigmoid(pre) * (1.0 / GLA_TAU)
    la_hi, la_lo = _split_bf16(la)
    tri = tri_ref[...]
    blk = blk_ref[...]
    tri_mask = tri.astype(f32) > 0.5
    g_gla = g_gla_ref[...]
    n_chunk = GLA_TILE // GLA_CHUNK

    def chunk_columns(a):
        cols = []
        for c in range(n_chunk):
            parts = []
            if c > 0:
                parts.append(jnp.zeros((c * GLA_CHUNK, GLA_DK), a.dtype))
            parts.append(a[c * GLA_CHUNK:(c + 1) * GLA_CHUNK])
            if c < n_chunk - 1:
                parts.append(jnp.zeros(((n_chunk - 1 - c) * GLA_CHUNK, GLA_DK), a.dtype))
            cols.append(jnp.concatenate(parts, axis=0))
        return jnp.concatenate(cols, axis=1)

    state = [st_ref[rd, hh] for hh in range(GLA_H)]
    for g in range(T // GLA_TILE):
        r0 = g * GLA_TILE
        lah = la_hi[r0:r0 + GLA_TILE]
        lal = la_lo[r0:r0 + GLA_TILE]
        b = _dot(tri, lah) + _dot(tri, lal)
        bl = _dot(blk, lah) + _dot(blk, lal)
        gq_g = gq[r0:r0 + GLA_TILE]
        gk_g = gk[r0:r0 + GLA_TILE]
        qd = (gq_g * (GLA_DK ** -0.5) * jnp.exp(b)).astype(bf16)
        kn = (gk_g * jnp.exp(-b)).astype(bf16)
        kd = (gk_g * jnp.exp(bl - b)).astype(bf16)
        dec = jnp.exp(bl)
        for hh in range(GLA_H):
            k0 = hh * GLA_DK
            v0 = hh * GLA_DV
            qd_h = qd[:, k0:k0 + GLA_DK]
            v_h = gvb[r0:r0 + GLA_TILE, v0:v0 + GLA_DV]
            att = jnp.where(tri_mask, _dot_nt(qd_h, kn[:, k0:k0 + GLA_DK]), 0.0).astype(bf16)
            o = _dot(att, v_h)
            upd = _dot_tn(v_h, chunk_columns(kd[:, k0:k0 + GLA_DK]))
            st = state[hh]
            sts = []
            for c in range(n_chunk):
                sts.append(st.astype(bf16))
                st = (dec[c * GLA_CHUNK:c * GLA_CHUNK + 1, k0:k0 + GLA_DK] * st
                      + upd[:, c * GLA_DK:(c + 1) * GLA_DK])
            state[hh] = st
            o = o + _dot_nt(chunk_columns(qd_h), jnp.concatenate(sts, axis=1))
            on = o * lax.rsqrt(jnp.mean(o * o, axis=-1, keepdims=True) + EPS) * g_gla
            ya_ref[r0:r0 + GLA_TILE, v0:v0 + GLA_DV] = (
                on * _silu(gz[r0:r0 + GLA_TILE, v0:v0 + GLA_DV])).astype(bf16)
    for hh in range(GLA_H):
        st_ref[wr, hh] = state[hh]
    y_a = _dot(ya_ref[...], w_br_gla_ref[...])
    merged = gate(0) * y_a

    sq = proj(C_SQ, SWA_Q)
    sk = proj(C_SK, SWA_KV)
    vb = proj(C_SV, SWA_KV).astype(bf16)
    cs = cos_ref[...]
    sn = sin_ref[...]
    qscale = SWA_HD ** -0.5
    half = SWA_Q // 2
    q_cols = []
    for c in range(SWA_G):
        q1 = sq[:, c * LANES:(c + 1) * LANES]
        q2 = sq[:, half + c * LANES:half + (c + 1) * LANES]
        q_cols.append(jnp.concatenate([((q1 * cs - q2 * sn) * qscale).astype(bf16),
                                       ((q2 * cs + q1 * sn) * qscale).astype(bf16)], axis=1))
    k1 = sk[:, :LANES]
    k2 = sk[:, LANES:]
    kr = jnp.concatenate([k1 * cs - k2 * sn, k2 * cs + k1 * sn], axis=1).astype(bf16)
    klane = lax.broadcasted_iota(jnp.int32, (1, SWA_KV), 1)
    kgrp = (klane % LANES) // SWA_HALF
    vgrp = klane // SWA_HD
    zero_b = jnp.zeros((), bf16)
    for n in range(T // WINDOW):
        r0 = n * WINDOW
        if n == 0:
            kp = kprev_ref[rd]
            vp = vprev_ref[rd]
            bias = bias_ref[jnp.where(s_idx == 0, 1, 0)]
        else:
            kp = kr[r0 - WINDOW:r0]
            vp = vb[r0 - WINDOW:r0]
            bias = bias_ref[0]
        kcat = jnp.concatenate([kp, kr[r0:r0 + WINDOW]], axis=0)
        vcat = jnp.concatenate([vp, vb[r0:r0 + WINDOW]], axis=0)
        rhs = jnp.concatenate([jnp.where(kgrp == j, kcat, zero_b) for j in range(SWA_HKV)], axis=0)
        vstk = jnp.concatenate([jnp.where(vgrp == j, vcat, zero_b) for j in range(SWA_HKV)], axis=0)
        lhs = jnp.concatenate([q_cols[c][r0:r0 + WINDOW] for c in range(SWA_G)], axis=0)
        s = _dot_nt(lhs, rhs)
        for c in range(SWA_G):
            for j in range(SWA_HKV):
                sc = s[c * WINDOW:(c + 1) * WINDOW, j * 2 * WINDOW:(j + 1) * 2 * WINDOW] + bias
                sink = sinks_ref[j * SWA_G + c]
                m = jnp.maximum(jnp.max(sc, axis=-1, keepdims=True), sink)
                p = jnp.exp(sc - m)
                l = jnp.sum(p, axis=-1, keepdims=True) + jnp.exp(sink - m)
                p_ref[n, c * WINDOW:(c + 1) * WINDOW, j * 2 * WINDOW:(j + 1) * 2 * WINDOW] = (
                    p * (1.0 / l)).astype(bf16)
        o = _dot(p_ref[n], vstk)
        for c in range(SWA_G):
            ob_ref[r0:r0 + WINDOW, c * SWA_KV:(c + 1) * SWA_KV] = o[c * WINDOW:(c + 1) * WINDOW, :]
    kprev_ref[wr] = kr[T - WINDOW:T]
    vprev_ref[wr] = vb[T - WINDOW:T]
    sz = proj(C_SZ, SWA_Q)
    y_b = _dot((ob_ref[...] * _silu(sz)).astype(bf16), w_br_swa_ref[...])
    merged = merged + gate(1) * y_b

    xq = proj(C_XQ, X_W)
    for hh in range(X_H):
        c0 = hh * X_HD
        qh = (xq[:, c0:c0 + X_HD] * (X_HD ** -0.5)).astype(bf16)
        sc = _dot_nt(qh, mkv_ref[0, :, c0:c0 + X_HD])
        m = jnp.max(sc, axis=-1, keepdims=True)
        p = jnp.exp(sc - m)
        l = jnp.sum(p, axis=-1, keepdims=True)
        oc_ref[:, c0:c0 + X_HD] = _dot((p * (1.0 / l)).astype(bf16),
                                       mkv_ref[0, :, X_W + c0:X_W + c0 + X_HD])
    xz = proj(C_XZ, X_W)
    y_c = _dot((oc_ref[...] * _silu(xz)).astype(bf16), w_br_mem_ref[...])
    merged = merged + gate(2) * y_c

    xo = x + _dot(merged.astype(bf16), w_out_ref[...])
    if apply_final_norm:
        xo = xo * lax.rsqrt(jnp.mean(xo * xo, axis=-1, keepdims=True) + EPS) * g_final_ref[...]
    out_ref[0] = xo


def _seq_tile(S):
    for t in (256, 128):
        if S % t == 0:
            return t
    raise ValueError(f"sequence length {S} must be a multiple of {WINDOW}")


def _layer(x, mkv, cos_t, sin_t, w_in, b_in, w_up, b_gate, g_mix, g_gla, sinks, w_br_gla, w_br_swa,
           w_br_mem, w_out, g_final, apply_final_norm):
    B, S, D = x.shape
    T = _seq_tile(S)
    M = mkv.shape[1]
    n_tiles = S // T

    rows = np.arange(GLA_TILE)
    same_chunk = (rows[:, None] // GLA_CHUNK) == (rows[None, :] // GLA_CHUNK)
    tri = jnp.asarray(same_chunk & (rows[None, :] <= rows[:, None]), jnp.bfloat16)
    blk = jnp.asarray(same_chunk, jnp.bfloat16)
    qi = np.arange(WINDOW)[:, None]
    kj = np.arange(2 * WINDOW)[None, :]
    band = (kj <= qi + WINDOW) & (kj > qi)
    bias_np = np.stack([np.where(band, 0.0, -np.inf),
                        np.where(band & (kj >= WINDOW), 0.0, -np.inf)]).astype(np.float32)
    bias = jnp.asarray(bias_np)

    def const(shape):
        nd = len(shape)
        return pl.BlockSpec(shape, lambda b, s: (0,) * nd, pipeline_mode=pl.Buffered(1))

    in_specs = [
        pl.BlockSpec(memory_space=pltpu.SMEM),
        pl.BlockSpec((1, T, D), lambda b, s: (b, s, 0)),
        pl.BlockSpec((1, M, 2 * X_W), lambda b, s: (b, 0, 0)),
        pl.BlockSpec((T, LANES), lambda b, s: (s, 0)),
        pl.BlockSpec((T, LANES), lambda b, s: (s, 0)),
        const((D, C_TOTAL)),
        const((1, C_TOTAL)),
        const((GLR_PAD, GLA_QK)),
        const((1, GLA_QK)),
        const((1, D)),
        const((1, GLA_DV)),
        const((GLA_V, D)),
        const((SWA_Q, D)),
        const((X_W, D)),
        const((D, D)),
        const((1, D)),
        const((GLA_TILE, GLA_TILE)),
        const((GLA_TILE, GLA_TILE)),
        const((2, WINDOW, 2 * WINDOW)),
    ]
    scratch = [
        pltpu.VMEM((2, GLA_H, GLA_DV, GLA_DK), jnp.float32),
        pltpu.VMEM((2, WINDOW, SWA_KV), jnp.bfloat16),
        pltpu.VMEM((2, WINDOW, SWA_KV), jnp.bfloat16),
        pltpu.VMEM((T, GLA_V), jnp.bfloat16),
        pltpu.VMEM((T, SWA_Q), jnp.float32),
        pltpu.VMEM((T, X_W), jnp.float32),
        pltpu.VMEM((T // WINDOW, SWA_G * WINDOW, SWA_HKV * 2 * WINDOW), jnp.bfloat16),
    ]
    kern = functools.partial(_layer_kernel, seq_tile=T, apply_final_norm=apply_final_norm)
    return pl.pallas_call(
        kern,
        grid=(B, n_tiles),
        in_specs=in_specs,
        out_specs=pl.BlockSpec((1, T, D), lambda b, s: (b, s, 0)),
        out_shape=jax.ShapeDtypeStruct((B, S, D), jnp.float32),
        scratch_shapes=scratch,
        compiler_params=pltpu.CompilerParams(
            dimension_semantics=("arbitrary", "arbitrary"),
            vmem_limit_bytes=56 * 1024 * 1024),
        name="layer",
    )(sinks, x, mkv, cos_t, sin_t, w_in, b_in, w_up, b_gate, g_mix, g_gla, w_br_gla, w_br_swa,
      w_br_mem, w_out, g_final, tri, blk, bias)


def kernel(x, mem, g_mix, g_mem, w_in, b_in, w_gla_gate_up, b_gla_gate, g_gla_norm, sinks, w_mem_kv,
           w_br_gla, w_br_swa, w_br_mem, w_out, g_final):
    B, S, D = x.shape
    depth = w_in.shape[0]
    bf16 = jnp.bfloat16

    inv = ROPE_THETA ** (-jnp.arange(SWA_HALF, dtype=jnp.float32) / SWA_HALF)
    ang = jnp.arange(S).astype(jnp.float32)[:, None] * inv[None, :]
    reps = LANES // SWA_HALF
    cos_t = jnp.tile(jnp.cos(ang), (1, reps))
    sin_t = jnp.tile(jnp.sin(ang), (1, reps))

    for l in range(depth):
        w_in_k = _kernel_columns(w_in[l].astype(bf16))
        b_in_k = _kernel_columns(b_in[l]).reshape(1, C_TOTAL)
        w_up = jnp.zeros((GLR_PAD, GLA_QK), bf16).at[:GLA_RANK].set(w_gla_gate_up[l].astype(bf16))
        w_br_swa_k = jnp.transpose(
            w_br_swa[l].astype(bf16).reshape(SWA_HKV, SWA_G, SWA_HD, D), (1, 0, 2, 3)).reshape(SWA_Q, D)
        mkv = _mem_kv(mem, g_mem[l], w_mem_kv[l].astype(bf16))
        x = _layer(
            x, mkv, cos_t, sin_t, w_in_k, b_in_k, w_up, b_gla_gate[l].reshape(1, GLA_QK),
            g_mix[l].reshape(1, D), g_gla_norm[l].reshape(1, GLA_DV), sinks[l],
            w_br_gla[l].astype(bf16), w_br_swa_k,
            w_br_mem[l].astype(bf16), w_out[l].astype(bf16), g_final.reshape(1, D),
            apply_final_norm=(l == depth - 1))
    return x
```

```python
import functools

import numpy as np
import jax
import jax.numpy as jnp
from jax import lax
from jax.experimental import pallas as pl
from jax.experimental.pallas import tpu as pltpu

D_MODEL = 1024
MEM_LEN = 256
GLA_H = 4
GLA_QK = 512
GLA_V = 1024
GLA_DK = 128
GLA_DV = 256
GLA_RANK = 16
GLA_TAU = 16.0
GLA_CHUNK = 64
GLA_TILE = 256
SWA_HD = 64
SWA_HALF = SWA_HD // 2
SWA_HQ = 16
SWA_HKV = 4
SWA_G = SWA_HQ // SWA_HKV
SWA_Q = 1024
SWA_KV = 256
WINDOW = 128
ROPE_THETA = 10000.0
X_H = 4
X_HD = 256
X_W = 1024
N_BRANCH = 3
EPS = 1e-6

LANES = 128

_IN_SIZES = (GLA_QK, GLA_QK, GLA_V, GLA_V, GLA_RANK, SWA_Q, SWA_KV, SWA_KV, SWA_Q, X_W, X_W,
             N_BRANCH * D_MODEL)
_IN_OFF = np.concatenate([[0], np.cumsum(_IN_SIZES)]).astype(np.int64)
(_R_GQ, _R_GK, _R_GV, _R_GZ, _R_GLR, _R_SQ, _R_SK, _R_SV, _R_SZ, _R_XQ, _R_XZ, _R_GATES) = (
    int(o) for o in _IN_OFF[:-1])

GLR_PAD = LANES
C_GQ = 0
C_GK = C_GQ + GLA_QK
C_GV = C_GK + GLA_QK
C_GZ = C_GV + GLA_V
C_GLR = C_GZ + GLA_V
C_SQ = C_GLR + GLR_PAD
C_SK = C_SQ + SWA_Q
C_SV = C_SK + SWA_KV
C_SZ = C_SV + SWA_KV
C_XQ = C_SZ + SWA_Q
C_XZ = C_XQ + X_W
C_GATES = C_XZ + X_W
C_TOTAL = C_GATES + N_BRANCH * D_MODEL


def _swa_q_cols(a):
    lead = a.shape[:-1]
    nl = len(lead)
    a = a.reshape(lead + (SWA_HKV, SWA_G, 2, SWA_HALF))
    a = jnp.transpose(a, tuple(range(nl)) + (nl + 2, nl + 1, nl, nl + 3))
    return a.reshape(lead + (SWA_Q,))


def _swa_k_cols(a):
    lead = a.shape[:-1]
    nl = len(lead)
    a = a.reshape(lead + (SWA_HKV, 2, SWA_HALF))
    a = jnp.transpose(a, tuple(range(nl)) + (nl + 1, nl, nl + 2))
    return a.reshape(lead + (SWA_KV,))


def _swa_o_cols(a):
    lead = a.shape[:-1]
    nl = len(lead)
    a = a.reshape(lead + (SWA_HKV, SWA_G, SWA_HD))
    a = jnp.transpose(a, tuple(range(nl)) + (nl + 1, nl, nl + 2))
    return a.reshape(lead + (SWA_Q,))


def _kernel_columns(a):
    lead = a.shape[:-1]
    seg = lambda off, n: lax.slice_in_dim(a, off, off + n, axis=a.ndim - 1)
    mid = jnp.concatenate([
        seg(_R_GLR, GLA_RANK),
        jnp.zeros(lead + (GLR_PAD - GLA_RANK,), a.dtype),
        _swa_q_cols(seg(_R_SQ, SWA_Q)),
        _swa_k_cols(seg(_R_SK, SWA_KV)),
        seg(_R_SV, SWA_KV),
        _swa_o_cols(seg(_R_SZ, SWA_Q)),
    ], axis=-1)
    assert mid.shape[-1] == C_XQ - C_GLR
    return seg(_R_GQ, C_GLR), mid, seg(_R_XQ, C_TOTAL - C_XQ)


def _dot(a, b):
    return jnp.dot(a, b, preferred_element_type=jnp.float32)


def _dot_nt(a, b):
    return lax.dot_general(a, b, (((1,), (1,)), ((), ())), preferred_element_type=jnp.float32)


def _dot_tn(a, b):
    return lax.dot_general(a, b, (((0,), (0,)), ((), ())), preferred_element_type=jnp.float32)


def _sigmoid(x):
    return 1.0 / (1.0 + jnp.exp(-x))


def _silu(x):
    return x * _sigmoid(x)


def _log_sigmoid(x):
    return jnp.minimum(x, 0.0) - jnp.log(1.0 + jnp.exp(-jnp.abs(x)))


def _split_bf16(x):
    hi = x.astype(jnp.bfloat16)
    lo = (x - hi.astype(jnp.float32)).astype(jnp.bfloat16)
    return hi, lo


def _mem_kv_kernel(mem_ref, g_ref, w_ref, out_ref):
    m = mem_ref[0]
    y = m * lax.rsqrt(jnp.mean(m * m, axis=-1, keepdims=True) + EPS) * g_ref[...]
    out_ref[0] = _dot(y.astype(jnp.bfloat16), w_ref[...]).astype(jnp.bfloat16)


def _mem_kv(mem, g_mem, w_mem_kv_bf16):
    B, M, D = mem.shape
    N = w_mem_kv_bf16.shape[1]
    return pl.pallas_call(
        _mem_kv_kernel,
        grid=(B,),
        in_specs=[
            pl.BlockSpec((1, M, D), lambda b: (b, 0, 0)),
            pl.BlockSpec((1, D), lambda b: (0, 0)),
            pl.BlockSpec((D, N), lambda b: (0, 0)),
        ],
        out_specs=pl.BlockSpec((1, M, N), lambda b: (b, 0, 0)),
        out_shape=jax.ShapeDtypeStruct((B, M, N), jnp.bfloat16),
        compiler_params=pltpu.CompilerParams(dimension_semantics=("arbitrary",)),
        name="mem_kv",
    )(mem, g_mem.reshape(1, D), w_mem_kv_bf16)


def _layer_kernel(sinks_ref, x_ref, mkv_ref, cos_ref, sin_ref, w_lo_ref, w_mid_ref, w_hi_ref, b_in_ref,
                  w_up_ref,
                  b_gate_ref, g_mix_ref, g_gla_ref, w_br_gla_ref, w_br_swa_ref, w_br_mem_ref,
                  w_out_ref, g_final_ref, tri_ref, blk_ref, bias_ref,
                  out_ref,
                  st_ref, kprev_ref, vprev_ref, ya_ref, ob_ref, oc_ref, p_ref,
                  *, seq_tile, apply_final_norm):
    T = seq_tile
    f32 = jnp.float32
    bf16 = jnp.bfloat16
    s_idx = pl.program_id(1)

    rd = s_idx % 2
    wr = 1 - rd

    @pl.when(s_idx == 0)
    def _():
        st_ref[0] = jnp.zeros(st_ref.shape[1:], st_ref.dtype)
        kprev_ref[0] = jnp.zeros(kprev_ref.shape[1:], kprev_ref.dtype)
        vprev_ref[0] = jnp.zeros(vprev_ref.shape[1:], vprev_ref.dtype)

    x = x_ref[0]
    h = x * lax.rsqrt(jnp.mean(x * x, axis=-1, keepdims=True) + EPS) * g_mix_ref[...]
    hb = h.astype(bf16)

    def proj(lo, width):
        if lo < C_GLR:
            w = w_lo_ref[:, lo:lo + width]
        elif lo < C_XQ:
            w = w_mid_ref[:, lo - C_GLR:lo - C_GLR + width]
        else:
            w = w_hi_ref[:, lo - C_XQ:lo - C_XQ + width]
        return _dot(hb, w) + b_in_ref[:, lo:lo + width]

    def gate(i):
        return _sigmoid(proj(C_GATES + i * D_MODEL, D_MODEL))

    gq = proj(C_GQ, GLA_QK)
    gk = proj(C_GK, GLA_QK)
    gvb = proj(C_GV, GLA_V).astype(bf16)
    gz = proj(C_GZ, GLA_V)
    glr = proj(C_GLR, GLR_PAD)
    pre = _dot(glr.astype(bf16), w_up_ref[...]) + b_gate_ref[...]
    la = _log_sigmoid(pre) * (1.0 / GLA_TAU)
    la_hi, la_lo = _split_bf16(la)
    tri = tri_ref[...]
    blk = blk_ref[...]
    tri_mask = tri.astype(f32) > 0.5
    g_gla = g_gla_ref[...]
    n_chunk = GLA_TILE // GLA_CHUNK

    def chunk_columns(a):
        cols = []
        for c in range(n_chunk):
            parts = []
            if c > 0:
                parts.append(jnp.zeros((c * GLA_CHUNK, GLA_DK), a.dtype))
            parts.append(a[c * GLA_CHUNK:(c + 1) * GLA_CHUNK])
            if c < n_chunk - 1:
                parts.append(jnp.zeros(((n_chunk - 1 - c) * GLA_CHUNK, GLA_DK), a.dtype))
            cols.append(jnp.concatenate(parts, axis=0))
        return jnp.concatenate(cols, axis=1)

    state = [st_ref[rd, hh] for hh in range(GLA_H)]
    for g in range(T // GLA_TILE):
        r0 = g * GLA_TILE
        lah = la_hi[r0:r0 + GLA_TILE]
        lal = la_lo[r0:r0 + GLA_TILE]
        b = _dot(tri, lah) + _dot(tri, lal)
        bl = _dot(blk, lah) + _dot(blk, lal)
        gq_g = gq[r0:r0 + GLA_TILE]
        gk_g = gk[r0:r0 + GLA_TILE]
        qd = (gq_g * (GLA_DK ** -0.5) * jnp.exp(b)).astype(bf16)
        kn = (gk_g * jnp.exp(-b)).astype(bf16)
        kd = (gk_g * jnp.exp(bl - b)).astype(bf16)
        dec = jnp.exp(bl)
        for hh in range(GLA_H):
            k0 = hh * GLA_DK
            v0 = hh * GLA_DV
            qd_h = qd[:, k0:k0 + GLA_DK]
            v_h = gvb[r0:r0 + GLA_TILE, v0:v0 + GLA_DV]
            att = jnp.where(tri_mask, _dot_nt(qd_h, kn[:, k0:k0 + GLA_DK]), 0.0).astype(bf16)
            o = _dot(att, v_h)
            upd = _dot_tn(v_h, chunk_columns(kd[:, k0:k0 + GLA_DK]))
            st = state[hh]
            sts = []
            for c in range(n_chunk):
                sts.append(st.astype(bf16))
                st = (dec[c * GLA_CHUNK:c * GLA_CHUNK + 1, k0:k0 + GLA_DK] * st
                      + upd[:, c * GLA_DK:(c + 1) * GLA_DK])
            state[hh] = st
            o = o + _dot_nt(chunk_columns(qd_h), jnp.concatenate(sts, axis=1))
            on = o * lax.rsqrt(jnp.mean(o * o, axis=-1, keepdims=True) + EPS) * g_gla
            ya_ref[r0:r0 + GLA_TILE, v0:v0 + GLA_DV] = (
                on * _silu(gz[r0:r0 + GLA_TILE, v0:v0 + GLA_DV])).astype(bf16)
    for hh in range(GLA_H):
        st_ref[wr, hh] = state[hh]
    y_a = _dot(ya_ref[...], w_br_gla_ref[...])
    merged = gate(0) * y_a

    sq = proj(C_SQ, SWA_Q)
    sk = proj(C_SK, SWA_KV)
    vb = proj(C_SV, SWA_KV).astype(bf16)
    cs = cos_ref[...]
    sn = sin_ref[...]
    qscale = SWA_HD ** -0.5
    half = SWA_Q // 2
    q_cols = []
    for c in range(SWA_G):
        q1 = sq[:, c * LANES:(c + 1) * LANES]
        q2 = sq[:, half + c * LANES:half + (c + 1) * LANES]
        q_cols.append(jnp.concatenate([((q1 * cs - q2 * sn) * qscale).astype(bf16),
                                       ((q2 * cs + q1 * sn) * qscale).astype(bf16)], axis=1))
    k1 = sk[:, :LANES]
    k2 = sk[:, LANES:]
    kr = jnp.concatenate([k1 * cs - k2 * sn, k2 * cs + k1 * sn], axis=1).astype(bf16)
    klane = lax.broadcasted_iota(jnp.int32, (1, SWA_KV), 1)
    kgrp = (klane % LANES) // SWA_HALF
    vgrp = klane // SWA_HD
    zero_b = jnp.zeros((), bf16)
    for n in range(T // WINDOW):
        r0 = n * WINDOW
        if n == 0:
            kp = kprev_ref[rd]
            vp = vprev_ref[rd]
            bias = bias_ref[jnp.where(s_idx == 0, 1, 0)]
        else:
            kp = kr[r0 - WINDOW:r0]
            vp = vb[r0 - WINDOW:r0]
            bias = bias_ref[0]
        kcat = jnp.concatenate([kp, kr[r0:r0 + WINDOW]], axis=0)
        vcat = jnp.concatenate([vp, vb[r0:r0 + WINDOW]], axis=0)
        rhs = jnp.concatenate([jnp.where(kgrp == j, kcat, zero_b) for j in range(SWA_HKV)], axis=0)
        vstk = jnp.concatenate([jnp.where(vgrp == j, vcat, zero_b) for j in range(SWA_HKV)], axis=0)
        lhs = jnp.concatenate([q_cols[c][r0:r0 + WINDOW] for c in range(SWA_G)], axis=0)
        s = _dot_nt(lhs, rhs)
        for c in range(SWA_G):
            for j in range(SWA_HKV):
                sc = s[c * WINDOW:(c + 1) * WINDOW, j * 2 * WINDOW:(j + 1) * 2 * WINDOW] + bias
                sink = sinks_ref[j * SWA_G + c]
                m = jnp.maximum(jnp.max(sc, axis=-1, keepdims=True), sink)
                p = jnp.exp(sc - m)
                l = jnp.sum(p, axis=-1, keepdims=True) + jnp.exp(sink - m)
                p_ref[n, c * WINDOW:(c + 1) * WINDOW, j * 2 * WINDOW:(j + 1) * 2 * WINDOW] = (
                    p * (1.0 / l)).astype(bf16)
        o = _dot(p_ref[n], vstk)
        for c in range(SWA_G):
            ob_ref[r0:r0 + WINDOW, c * SWA_KV:(c + 1) * SWA_KV] = o[c * WINDOW:(c + 1) * WINDOW, :]
    kprev_ref[wr] = kr[T - WINDOW:T]
    vprev_ref[wr] = vb[T - WINDOW:T]
    sz = proj(C_SZ, SWA_Q)
    y_b = _dot((ob_ref[...] * _silu(sz)).astype(bf16), w_br_swa_ref[...])
    merged = merged + gate(1) * y_b

    xq = proj(C_XQ, X_W)
    for hh in range(X_H):
        c0 = hh * X_HD
        qh = (xq[:, c0:c0 + X_HD] * (X_HD ** -0.5)).astype(bf16)
        sc = _dot_nt(qh, mkv_ref[0, :, c0:c0 + X_HD])
        m = jnp.max(sc, axis=-1, keepdims=True)
        p = jnp.exp(sc - m)
        l = jnp.sum(p, axis=-1, keepdims=True)
        oc_ref[:, c0:c0 + X_HD] = _dot((p * (1.0 / l)).astype(bf16),
                                       mkv_ref[0, :, X_W + c0:X_W + c0 + X_HD])
    xz = proj(C_XZ, X_W)
    y_c = _dot((oc_ref[...] * _silu(xz)).astype(bf16), w_br_mem_ref[...])
    merged = merged + gate(2) * y_c

    xo = x + _dot(merged.astype(bf16), w_out_ref[...])
    if apply_final_norm:
        xo = xo * lax.rsqrt(jnp.mean(xo * xo, axis=-1, keepdims=True) + EPS) * g_final_ref[...]
    out_ref[0] = xo


def _seq_tile(S):
    for t in (512, 256, 128):
        if S % t == 0:
            return t
    raise ValueError(f"sequence length {S} must be a multiple of {WINDOW}")


def _layer(x, mkv, cos_t, sin_t, w_in_parts, b_in, w_up, b_gate, g_mix, g_gla, sinks, w_br_gla, w_br_swa,
           w_br_mem, w_out, g_final, apply_final_norm):
    B, S, D = x.shape
    T = _seq_tile(S)
    M = mkv.shape[1]
    n_tiles = S // T

    rows = np.arange(GLA_TILE)
    same_chunk = (rows[:, None] // GLA_CHUNK) == (rows[None, :] // GLA_CHUNK)
    tri = jnp.asarray(same_chunk & (rows[None, :] <= rows[:, None]), jnp.bfloat16)
    blk = jnp.asarray(same_chunk, jnp.bfloat16)
    qi = np.arange(WINDOW)[:, None]
    kj = np.arange(2 * WINDOW)[None, :]
    band = (kj <= qi + WINDOW) & (kj > qi)
    bias_np = np.stack([np.where(band, 0.0, -np.inf),
                        np.where(band & (kj >= WINDOW), 0.0, -np.inf)]).astype(np.float32)
    bias = jnp.asarray(bias_np)

    def const(shape):
        nd = len(shape)
        return pl.BlockSpec(shape, lambda b, s: (0,) * nd, pipeline_mode=pl.Buffered(1))

    in_specs = [
        pl.BlockSpec(memory_space=pltpu.SMEM),
        pl.BlockSpec((1, T, D), lambda b, s: (b, s, 0)),
        pl.BlockSpec((1, M, 2 * X_W), lambda b, s: (b, 0, 0)),
        pl.BlockSpec((T, LANES), lambda b, s: (s, 0)),
        pl.BlockSpec((T, LANES), lambda b, s: (s, 0)),
        const((D, C_GLR)),
        const((D, C_XQ - C_GLR)),
        const((D, C_TOTAL - C_XQ)),
        const((1, C_TOTAL)),
        const((GLR_PAD, GLA_QK)),
        const((1, GLA_QK)),
        const((1, D)),
        const((1, GLA_DV)),
        const((GLA_V, D)),
        const((SWA_Q, D)),
        const((X_W, D)),
        const((D, D)),
        const((1, D)),
        const((GLA_TILE, GLA_TILE)),
        const((GLA_TILE, GLA_TILE)),
        const((2, WINDOW, 2 * WINDOW)),
    ]
    scratch = [
        pltpu.VMEM((2, GLA_H, GLA_DV, GLA_DK), jnp.float32),
        pltpu.VMEM((2, WINDOW, SWA_KV), jnp.bfloat16),
        pltpu.VMEM((2, WINDOW, SWA_KV), jnp.bfloat16),
        pltpu.VMEM((T, GLA_V), jnp.bfloat16),
        pltpu.VMEM((T, SWA_Q), jnp.float32),
        pltpu.VMEM((T, X_W), jnp.float32),
        pltpu.VMEM((T // WINDOW, SWA_G * WINDOW, SWA_HKV * 2 * WINDOW), jnp.bfloat16),
    ]
    kern = functools.partial(_layer_kernel, seq_tile=T, apply_final_norm=apply_final_norm)
    return pl.pallas_call(
        kern,
        grid=(B, n_tiles),
        in_specs=in_specs,
        out_specs=pl.BlockSpec((1, T, D), lambda b, s: (b, s, 0)),
        out_shape=jax.ShapeDtypeStruct((B, S, D), jnp.float32),
        scratch_shapes=scratch,
        compiler_params=pltpu.CompilerParams(
            dimension_semantics=("arbitrary", "arbitrary"),
            vmem_limit_bytes=60 * 1024 * 1024),
        name="layer",
    )(sinks, x, mkv, cos_t, sin_t, *w_in_parts, b_in, w_up, b_gate, g_mix, g_gla, w_br_gla, w_br_swa,
      w_br_mem, w_out, g_final, tri, blk, bias)


def kernel(x, mem, g_mix, g_mem, w_in, b_in, w_gla_gate_up, b_gla_gate, g_gla_norm, sinks, w_mem_kv,
           w_br_gla, w_br_swa, w_br_mem, w_out, g_final):
    B, S, D = x.shape
    depth = w_in.shape[0]
    bf16 = jnp.bfloat16

    inv = ROPE_THETA ** (-jnp.arange(SWA_HALF, dtype=jnp.float32) / SWA_HALF)
    ang = jnp.arange(S).astype(jnp.float32)[:, None] * inv[None, :]
    reps = LANES // SWA_HALF
    cos_t = jnp.tile(jnp.cos(ang), (1, reps))
    sin_t = jnp.tile(jnp.sin(ang), (1, reps))

    for l in range(depth):
        w_in_k = tuple(p.astype(bf16) for p in _kernel_columns(w_in[l]))
        b_in_k = jnp.concatenate(_kernel_columns(b_in[l])).reshape(1, C_TOTAL)
        w_up = jnp.zeros((GLR_PAD, GLA_QK), bf16).at[:GLA_RANK].set(w_gla_gate_up[l].astype(bf16))
        w_br_swa_k = jnp.transpose(
            w_br_swa[l].astype(bf16).reshape(SWA_HKV, SWA_G, SWA_HD, D), (1, 0, 2, 3)).reshape(SWA_Q, D)
        mkv = _mem_kv(mem, g_mem[l], w_mem_kv[l].astype(bf16))
        x = _layer(
            x, mkv, cos_t, sin_t, w_in_k, b_in_k, w_up, b_gla_gate[l].reshape(1, GLA_QK),
            g_mix[l].reshape(1, D), g_gla_norm[l].reshape(1, GLA_DV), sinks[l],
            w_br_gla[l].astype(bf16), w_br_swa_k,
            w_br_mem[l].astype(bf16), w_out[l].astype(bf16), g_final.reshape(1, D),
            apply_final_norm=(l == depth - 1))
    return x
```

```python
import functools

import numpy as np
import jax
import jax.numpy as jnp
from jax import lax
from jax.experimental import pallas as pl
from jax.experimental.pallas import tpu as pltpu

D_MODEL = 1024
MEM_LEN = 256
GLA_H = 4
GLA_QK = 512
GLA_V = 1024
GLA_DK = 128
GLA_DV = 256
GLA_RANK = 16
GLA_TAU = 16.0
GLA_CHUNK = 64
GLA_TILE = 256
SWA_HD = 64
SWA_HALF = SWA_HD // 2
SWA_HQ = 16
SWA_HKV = 4
SWA_G = SWA_HQ // SWA_HKV
SWA_Q = 1024
SWA_KV = 256
WINDOW = 128
ROPE_THETA = 10000.0
X_H = 4
X_HD = 256
X_W = 1024
N_BRANCH = 3
EPS = 1e-6

LANES = 128
MXU_N = 256

_IN_SIZES = (GLA_QK, GLA_QK, GLA_V, GLA_V, GLA_RANK, SWA_Q, SWA_KV, SWA_KV, SWA_Q, X_W, X_W,
             N_BRANCH * D_MODEL)
_IN_OFF = np.concatenate([[0], np.cumsum(_IN_SIZES)]).astype(np.int64)
(_R_GQ, _R_GK, _R_GV, _R_GZ, _R_GLR, _R_SQ, _R_SK, _R_SV, _R_SZ, _R_XQ, _R_XZ, _R_GATES) = (
    int(o) for o in _IN_OFF[:-1])

GLR_PAD = LANES
C_GQ = 0
C_GK = C_GQ + GLA_QK
C_GV = C_GK + GLA_QK
C_GZ = C_GV + GLA_V
C_GLR = C_GZ + GLA_V
C_SQ = C_GLR + GLR_PAD
C_SK = C_SQ + SWA_Q
C_SV = C_SK + SWA_KV
C_SZ = C_SV + SWA_KV
C_XQ = C_SZ + SWA_Q
C_XZ = C_XQ + X_W
C_GATES = C_XZ + X_W
C_TOTAL = C_GATES + N_BRANCH * D_MODEL

FILL_ORDER = ("gz", "sz", "g0", "xq", "xz", "g1", "g2")
FILL_PER_GLA_TILE = 4
FILL_SOFTMAX_PERIOD = 4


def _swa_q_cols(a):
    lead = a.shape[:-1]
    nl = len(lead)
    a = a.reshape(lead + (SWA_HKV, SWA_G, 2, SWA_HALF))
    a = jnp.transpose(a, tuple(range(nl)) + (nl + 2, nl + 1, nl, nl + 3))
    return a.reshape(lead + (SWA_Q,))


def _swa_k_cols(a):
    lead = a.shape[:-1]
    nl = len(lead)
    a = a.reshape(lead + (SWA_HKV, 2, SWA_HALF))
    a = jnp.transpose(a, tuple(range(nl)) + (nl + 1, nl, nl + 2))
    return a.reshape(lead + (SWA_KV,))


def _swa_o_cols(a):
    lead = a.shape[:-1]
    nl = len(lead)
    a = a.reshape(lead + (SWA_HKV, SWA_G, SWA_HD))
    a = jnp.transpose(a, tuple(range(nl)) + (nl + 1, nl, nl + 2))
    return a.reshape(lead + (SWA_Q,))


def _kernel_columns(a):
    lead = a.shape[:-1]
    seg = lambda off, n: lax.slice_in_dim(a, off, off + n, axis=a.ndim - 1)
    mid = jnp.concatenate([
        seg(_R_GLR, GLA_RANK),
        jnp.zeros(lead + (GLR_PAD - GLA_RANK,), a.dtype),
        _swa_q_cols(seg(_R_SQ, SWA_Q)),
        _swa_k_cols(seg(_R_SK, SWA_KV)),
        seg(_R_SV, SWA_KV),
        _swa_o_cols(seg(_R_SZ, SWA_Q)),
    ], axis=-1)
    assert mid.shape[-1] == C_XQ - C_GLR
    return seg(_R_GQ, C_GLR), mid, seg(_R_XQ, C_TOTAL - C_XQ)


def _dot(a, b):
    return jnp.dot(a, b, preferred_element_type=jnp.float32)


def _dot_nt(a, b):
    return lax.dot_general(a, b, (((1,), (1,)), ((), ())), preferred_element_type=jnp.float32)


def _dot_tn(a, b):
    return lax.dot_general(a, b, (((0,), (0,)), ((), ())), preferred_element_type=jnp.float32)


def _sigmoid(x):
    return 1.0 / (1.0 + jnp.exp(-x))


def _silu(x):
    return x * _sigmoid(x)


def _log_sigmoid(x):
    return jnp.minimum(x, 0.0) - jnp.log(1.0 + jnp.exp(-jnp.abs(x)))


def _split_bf16(x):
    hi = x.astype(jnp.bfloat16)
    lo = (x - hi.astype(jnp.float32)).astype(jnp.bfloat16)
    return hi, lo


def _mem_kv_kernel(mem_ref, g_ref, w_ref, out_ref):
    m = mem_ref[0]
    y = m * lax.rsqrt(jnp.mean(m * m, axis=-1, keepdims=True) + EPS) * g_ref[...]
    out_ref[0] = _dot(y.astype(jnp.bfloat16), w_ref[...]).astype(jnp.bfloat16)


def _mem_kv(mem, g_mem, w_mem_kv_bf16):
    B, M, D = mem.shape
    N = w_mem_kv_bf16.shape[1]
    return pl.pallas_call(
        _mem_kv_kernel,
        grid=(B,),
        in_specs=[
            pl.BlockSpec((1, M, D), lambda b: (b, 0, 0)),
            pl.BlockSpec((1, D), lambda b: (0, 0)),
            pl.BlockSpec((D, N), lambda b: (0, 0)),
        ],
        out_specs=pl.BlockSpec((1, M, N), lambda b: (b, 0, 0)),
        out_shape=jax.ShapeDtypeStruct((B, M, N), jnp.bfloat16),
        compiler_params=pltpu.CompilerParams(dimension_semantics=("arbitrary",)),
        name="mem_kv",
    )(mem, g_mem.reshape(1, D), w_mem_kv_bf16)


def _layer_kernel(sinks_ref, x_ref, mkv_ref, cos_ref, sin_ref, w_lo_ref, w_mid_ref, w_hi_ref, b_in_ref,
                  w_up_ref, b_gate_ref, g_mix_ref, g_gla_ref, w_br_gla_ref, w_br_swa_ref, w_br_mem_ref,
                  w_out_ref, g_final_ref, tri_ref, blk_ref, bias_ref,
                  out_ref,
                  st_ref, kprev_ref, vprev_ref, ya_ref, yb_ref, yc_ref,
                  *, seq_tile, apply_final_norm):
    T = seq_tile
    f32 = jnp.float32
    bf16 = jnp.bfloat16
    s_idx = pl.program_id(1)

    rd = s_idx % 2
    wr = 1 - rd

    @pl.when(s_idx == 0)
    def _():
        st_ref[0] = jnp.zeros(st_ref.shape[1:], st_ref.dtype)
        kprev_ref[0] = jnp.zeros(kprev_ref.shape[1:], kprev_ref.dtype)
        vprev_ref[0] = jnp.zeros(vprev_ref.shape[1:], vprev_ref.dtype)

    x = x_ref[0]
    h = x * lax.rsqrt(jnp.mean(x * x, axis=-1, keepdims=True) + EPS) * g_mix_ref[...]
    hb = h.astype(bf16)

    def proj(lo, width):
        if lo < C_GLR:
            w = w_lo_ref[:, lo:lo + width]
        elif lo < C_XQ:
            w = w_mid_ref[:, lo - C_GLR:lo - C_GLR + width]
        else:
            w = w_hi_ref[:, lo - C_XQ:lo - C_XQ + width]
        return _dot(hb, w) + b_in_ref[:, lo:lo + width]

    n_slice = D_MODEL // MXU_N
    late = {}
    late_specs = {"gz": (C_GZ, None), "sz": (C_SZ, _silu), "xq": (C_XQ, None), "xz": (C_XZ, _silu),
                  "g0": (C_GATES, _sigmoid), "g1": (C_GATES + D_MODEL, _sigmoid),
                  "g2": (C_GATES + 2 * D_MODEL, _sigmoid)}
    pending = [(name, i) for name in FILL_ORDER for i in range(n_slice)]

    def fill(count):
        for _ in range(min(count, len(pending))):
            name, i = pending.pop(0)
            lo, act = late_specs[name]
            v = proj(lo + i * MXU_N, MXU_N)
            late[name, i] = v if act is None else act(v)

    def need(name):
        while any((name, i) not in late for i in range(n_slice)):
            fill(1)

    def late_full(name):
        need(name)
        return jnp.concatenate([late[name, i] for i in range(n_slice)], axis=1)

    gq = proj(C_GQ, GLA_QK)
    gk = proj(C_GK, GLA_QK)
    gvb = proj(C_GV, GLA_V).astype(bf16)
    glr = proj(C_GLR, GLR_PAD)
    pre = _dot(glr.astype(bf16), w_up_ref[...]) + b_gate_ref[...]
    la = _log_sigmoid(pre) * (1.0 / GLA_TAU)
    la_hi, la_lo = _split_bf16(la)

    sq = proj(C_SQ, SWA_Q)
    sk = proj(C_SK, SWA_KV)
    vb = proj(C_SV, SWA_KV).astype(bf16)
    cs = cos_ref[...]
    sn = sin_ref[...]
    qscale = SWA_HD ** -0.5
    half = SWA_Q // 2
    q_cols = []
    for c in range(SWA_G):
        q1 = sq[:, c * LANES:(c + 1) * LANES]
        q2 = sq[:, half + c * LANES:half + (c + 1) * LANES]
        q_cols.append(jnp.concatenate([((q1 * cs - q2 * sn) * qscale).astype(bf16),
                                       ((q2 * cs + q1 * sn) * qscale).astype(bf16)], axis=1))
    k1 = sk[:, :LANES]
    k2 = sk[:, LANES:]
    kr = jnp.concatenate([k1 * cs - k2 * sn, k2 * cs + k1 * sn], axis=1).astype(bf16)
    kprev_ref[wr] = kr[T - WINDOW:T]
    vprev_ref[wr] = vb[T - WINDOW:T]

    tri = tri_ref[...]
    blk = blk_ref[...]
    tri_mask = tri.astype(f32) > 0.5
    g_gla = g_gla_ref[...]
    n_chunk = GLA_TILE // GLA_CHUNK

    def chunk_columns(a):
        cols = []
        for c in range(n_chunk):
            parts = []
            if c > 0:
                parts.append(jnp.zeros((c * GLA_CHUNK, GLA_DK), a.dtype))
            parts.append(a[c * GLA_CHUNK:(c + 1) * GLA_CHUNK])
            if c < n_chunk - 1:
                parts.append(jnp.zeros(((n_chunk - 1 - c) * GLA_CHUNK, GLA_DK), a.dtype))
            cols.append(jnp.concatenate(parts, axis=0))
        return jnp.concatenate(cols, axis=1)

    state = [st_ref[rd, hh] for hh in range(GLA_H)]
    for g in range(T // GLA_TILE):
        r0 = g * GLA_TILE
        lah = la_hi[r0:r0 + GLA_TILE]
        lal = la_lo[r0:r0 + GLA_TILE]
        b = _dot(tri, lah) + _dot(tri, lal)
        bl = _dot(blk, lah) + _dot(blk, lal)
        gq_g = gq[r0:r0 + GLA_TILE]
        gk_g = gk[r0:r0 + GLA_TILE]
        qd = (gq_g * (GLA_DK ** -0.5) * jnp.exp(b)).astype(bf16)
        kn = (gk_g * jnp.exp(-b)).astype(bf16)
        kd = (gk_g * jnp.exp(bl - b)).astype(bf16)
        dec = jnp.exp(bl)
        fill(FILL_PER_GLA_TILE)
        need("gz")
        for hh in range(GLA_H):
            k0 = hh * GLA_DK
            v0 = hh * GLA_DV
            qd_h = qd[:, k0:k0 + GLA_DK]
            v_h = gvb[r0:r0 + GLA_TILE, v0:v0 + GLA_DV]
            att = jnp.where(tri_mask, _dot_nt(qd_h, kn[:, k0:k0 + GLA_DK]), 0.0).astype(bf16)
            o = _dot(att, v_h)
            upd = _dot_tn(v_h, chunk_columns(kd[:, k0:k0 + GLA_DK]))
            st = state[hh]
            sts = []
            for c in range(n_chunk):
                sts.append(st.astype(bf16))
                st = (dec[c * GLA_CHUNK:c * GLA_CHUNK + 1, k0:k0 + GLA_DK] * st
                      + upd[:, c * GLA_DK:(c + 1) * GLA_DK])
            state[hh] = st
            o = o + _dot_nt(chunk_columns(qd_h), jnp.concatenate(sts, axis=1))
            on = o * lax.rsqrt(jnp.mean(o * o, axis=-1, keepdims=True) + EPS) * g_gla
            ya_ref[r0:r0 + GLA_TILE, v0:v0 + GLA_DV] = (
                on * _silu(late["gz", hh][r0:r0 + GLA_TILE])).astype(bf16)
    for hh in range(GLA_H):
        st_ref[wr, hh] = state[hh]
    need("sz")
    merged = late_full("g0") * _dot(ya_ref[...], w_br_gla_ref[...])

    klane = lax.broadcasted_iota(jnp.int32, (1, SWA_KV), 1)
    kgrp = (klane % LANES) // SWA_HALF
    vgrp = klane // SWA_HD
    zero_b = jnp.zeros((), bf16)
    for n in range(T // WINDOW):
        r0 = n * WINDOW
        if n == 0:
            kp = kprev_ref[rd]
            vp = vprev_ref[rd]
            bias = bias_ref[jnp.where(s_idx == 0, 1, 0)]
        else:
            kp = kr[r0 - WINDOW:r0]
            vp = vb[r0 - WINDOW:r0]
            bias = bias_ref[0]
        kcat = jnp.concatenate([kp, kr[r0:r0 + WINDOW]], axis=0)
        vcat = jnp.concatenate([vp, vb[r0:r0 + WINDOW]], axis=0)
        rhs = jnp.concatenate([jnp.where(kgrp == j, kcat, zero_b) for j in range(SWA_HKV)], axis=0)
        vstk = jnp.concatenate([jnp.where(vgrp == j, vcat, zero_b) for j in range(SWA_HKV)], axis=0)
        lhs = jnp.concatenate([q_cols[c][r0:r0 + WINDOW] for c in range(SWA_G)], axis=0)
        s = _dot_nt(lhs, rhs)
        for c in range(SWA_G):
            ps = []
            for j in range(SWA_HKV):
                sc = s[c * WINDOW:(c + 1) * WINDOW, j * 2 * WINDOW:(j + 1) * 2 * WINDOW] + bias
                sink = sinks_ref[j * SWA_G + c]
                m = jnp.maximum(jnp.max(sc, axis=-1, keepdims=True), sink)
                p = jnp.exp(sc - m)
                l = jnp.sum(p, axis=-1, keepdims=True) + jnp.exp(sink - m)
                ps.append((p * (1.0 / l)).astype(bf16))
                if (c * SWA_HKV + j) % FILL_SOFTMAX_PERIOD == FILL_SOFTMAX_PERIOD - 1:
                    fill(1)
            o = _dot(jnp.concatenate(ps, axis=1), vstk)
            yb_ref[r0:r0 + WINDOW, c * SWA_KV:(c + 1) * SWA_KV] = (
                o * late["sz", c][r0:r0 + WINDOW]).astype(bf16)
    merged = merged + late_full("g1") * _dot(yb_ref[...], w_br_swa_ref[...])

    need("xq")
    need("xz")
    for hh in range(X_H):
        c0 = hh * X_HD
        qh = (late["xq", hh] * (X_HD ** -0.5)).astype(bf16)
        sc = _dot_nt(qh, mkv_ref[0, :, c0:c0 + X_HD])
        m = jnp.max(sc, axis=-1, keepdims=True)
        p = jnp.exp(sc - m)
        l = jnp.sum(p, axis=-1, keepdims=True)
        o = _dot((p * (1.0 / l)).astype(bf16), mkv_ref[0, :, X_W + c0:X_W + c0 + X_HD])
        yc_ref[:, c0:c0 + X_HD] = (o * late["xz", hh]).astype(bf16)
    merged = merged + late_full("g2") * _dot(yc_ref[...], w_br_mem_ref[...])

    xo = x + _dot(merged.astype(bf16), w_out_ref[...])
    if apply_final_norm:
        xo = xo * lax.rsqrt(jnp.mean(xo * xo, axis=-1, keepdims=True) + EPS) * g_final_ref[...]
    out_ref[0] = xo


def _seq_tile(S):
    for t in (512, 256, 128):
        if S % t == 0:
            return t
    raise ValueError(f"sequence length {S} must be a multiple of {WINDOW}")


def _layer(x, mkv, cos_t, sin_t, w_in_parts, b_in, w_up, b_gate, g_mix, g_gla, sinks, w_br_gla, w_br_swa,
           w_br_mem, w_out, g_final, apply_final_norm):
    B, S, D = x.shape
    T = _seq_tile(S)
    M = mkv.shape[1]
    n_tiles = S // T

    rows = np.arange(GLA_TILE)
    same_chunk = (rows[:, None] // GLA_CHUNK) == (rows[None, :] // GLA_CHUNK)
    tri = jnp.asarray(same_chunk & (rows[None, :] <= rows[:, None]), jnp.bfloat16)
    blk = jnp.asarray(same_chunk, jnp.bfloat16)
    qi = np.arange(WINDOW)[:, None]
    kj = np.arange(2 * WINDOW)[None, :]
    band = (kj <= qi + WINDOW) & (kj > qi)
    bias_np = np.stack([np.where(band, 0.0, -np.inf),
                        np.where(band & (kj >= WINDOW), 0.0, -np.inf)]).astype(np.float32)
    bias = jnp.asarray(bias_np)

    def const(shape):
        nd = len(shape)
        return pl.BlockSpec(shape, lambda b, s: (0,) * nd, pipeline_mode=pl.Buffered(1))

    in_specs = [
        pl.BlockSpec(memory_space=pltpu.SMEM),
        pl.BlockSpec((1, T, D), lambda b, s: (b, s, 0)),
        pl.BlockSpec((1, M, 2 * X_W), lambda b, s: (b, 0, 0)),
        pl.BlockSpec((T, LANES), lambda b, s: (s, 0)),
        pl.BlockSpec((T, LANES), lambda b, s: (s, 0)),
        const((D, C_GLR)),
        const((D, C_XQ - C_GLR)),
        const((D, C_TOTAL - C_XQ)),
        const((1, C_TOTAL)),
        const((GLR_PAD, GLA_QK)),
        const((1, GLA_QK)),
        const((1, D)),
        const((1, GLA_DV)),
        const((GLA_V, D)),
        const((SWA_Q, D)),
        const((X_W, D)),
        const((D, D)),
        const((1, D)),
        const((GLA_TILE, GLA_TILE)),
        const((GLA_TILE, GLA_TILE)),
        const((2, WINDOW, 2 * WINDOW)),
    ]
    scratch = [
        pltpu.VMEM((2, GLA_H, GLA_DV, GLA_DK), jnp.float32),
        pltpu.VMEM((2, WINDOW, SWA_KV), jnp.bfloat16),
        pltpu.VMEM((2, WINDOW, SWA_KV), jnp.bfloat16),
        pltpu.VMEM((T, GLA_V), jnp.bfloat16),
        pltpu.VMEM((T, SWA_Q), jnp.bfloat16),
        pltpu.VMEM((T, X_W), jnp.bfloat16),
    ]
    kern = functools.partial(_layer_kernel, seq_tile=T, apply_final_norm=apply_final_norm)
    return pl.pallas_call(
        kern,
        grid=(B, n_tiles),
        in_specs=in_specs,
        out_specs=pl.BlockSpec((1, T, D), lambda b, s: (b, s, 0)),
        out_shape=jax.ShapeDtypeStruct((B, S, D), jnp.float32),
        scratch_shapes=scratch,
        compiler_params=pltpu.CompilerParams(
            dimension_semantics=("arbitrary", "arbitrary"),
            vmem_limit_bytes=60 * 1024 * 1024),
        name="layer",
    )(sinks, x, mkv, cos_t, sin_t, *w_in_parts, b_in, w_up, b_gate, g_mix, g_gla, w_br_gla, w_br_swa,
      w_br_mem, w_out, g_final, tri, blk, bias)


def kernel(x, mem, g_mix, g_mem, w_in, b_in, w_gla_gate_up, b_gla_gate, g_gla_norm, sinks, w_mem_kv,
           w_br_gla, w_br_swa, w_br_mem, w_out, g_final):
    B, S, D = x.shape
    depth = w_in.shape[0]
    bf16 = jnp.bfloat16

    inv = ROPE_THETA ** (-jnp.arange(SWA_HALF, dtype=jnp.float32) / SWA_HALF)
    ang = jnp.arange(S).astype(jnp.float32)[:, None] * inv[None, :]
    reps = LANES // SWA_HALF
    cos_t = jnp.tile(jnp.cos(ang), (1, reps))
    sin_t = jnp.tile(jnp.sin(ang), (1, reps))

    for l in range(depth):
        w_in_k = tuple(p.astype(bf16) for p in _kernel_columns(w_in[l]))
        b_in_k = jnp.concatenate(_kernel_columns(b_in[l])).reshape(1, C_TOTAL)
        w_up = jnp.zeros((GLR_PAD, GLA_QK), bf16).at[:GLA_RANK].set(w_gla_gate_up[l].astype(bf16))
        w_br_swa_k = jnp.transpose(
            w_br_swa[l].astype(bf16).reshape(SWA_HKV, SWA_G, SWA_HD, D), (1, 0, 2, 3)).reshape(SWA_Q, D)
        mkv = _mem_kv(mem, g_mem[l], w_mem_kv[l].astype(bf16))
        x = _layer(
            x, mkv, cos_t, sin_t, w_in_k, b_in_k, w_up, b_gla_gate[l].reshape(1, GLA_QK),
            g_mix[l].reshape(1, D), g_gla_norm[l].reshape(1, GLA_DV), sinks[l],
            w_br_gla[l].astype(bf16), w_br_swa_k,
            w_br_mem[l].astype(bf16), w_out[l].astype(bf16), g_final.reshape(1, D),
            apply_final_norm=(l == depth - 1))
    return x
```

```python
import functools

import numpy as np
import jax
import jax.numpy as jnp
from jax import lax
from jax.experimental import pallas as pl
from jax.experimental.pallas import tpu as pltpu

D_MODEL = 1024
MEM_LEN = 256
GLA_H = 4
GLA_QK = 512
GLA_V = 1024
GLA_DK = 128
GLA_DV = 256
GLA_RANK = 16
GLA_TAU = 16.0
GLA_CHUNK = 64
GLA_TILE = 256
SWA_HD = 64
SWA_HALF = SWA_HD // 2
SWA_HQ = 16
SWA_HKV = 4
SWA_G = SWA_HQ // SWA_HKV
SWA_Q = 1024
SWA_KV = 256
WINDOW = 128
ROPE_THETA = 10000.0
X_H = 4
X_HD = 256
X_W = 1024
N_BRANCH = 3
EPS = 1e-6

LANES = 128
MXU_N = 256

_IN_SIZES = (GLA_QK, GLA_QK, GLA_V, GLA_V, GLA_RANK, SWA_Q, SWA_KV, SWA_KV, SWA_Q, X_W, X_W,
             N_BRANCH * D_MODEL)
_IN_OFF = np.concatenate([[0], np.cumsum(_IN_SIZES)]).astype(np.int64)
(_R_GQ, _R_GK, _R_GV, _R_GZ, _R_GLR, _R_SQ, _R_SK, _R_SV, _R_SZ, _R_XQ, _R_XZ, _R_GATES) = (
    int(o) for o in _IN_OFF[:-1])

GLR_PAD = LANES
C_GQ = 0
C_GK = C_GQ + GLA_QK
C_GV = C_GK + GLA_QK
C_GZ = C_GV + GLA_V
C_GLR = C_GZ + GLA_V
C_SQ = C_GLR + GLR_PAD
C_SK = C_SQ + SWA_Q
C_SV = C_SK + SWA_KV
C_SZ = C_SV + SWA_KV
C_XQ = C_SZ + SWA_Q
C_XZ = C_XQ + X_W
C_GATES = C_XZ + X_W
C_TOTAL = C_GATES + N_BRANCH * D_MODEL

FILL_ORDER = ("gz", "sz", "g0", "xq", "xz", "g1", "g2")
FILL_PER_GLA_TILE = 4
FILL_SOFTMAX_PERIOD = 4
NORM_ROWS = 128
OUT_ROWS = 256
CAST_ROWS = 128


def _swa_q_cols(a):
    lead = a.shape[:-1]
    nl = len(lead)
    a = a.reshape(lead + (SWA_HKV, SWA_G, 2, SWA_HALF))
    a = jnp.transpose(a, tuple(range(nl)) + (nl + 2, nl + 1, nl, nl + 3))
    return a.reshape(lead + (SWA_Q,))


def _swa_k_cols(a):
    lead = a.shape[:-1]
    nl = len(lead)
    a = a.reshape(lead + (SWA_HKV, 2, SWA_HALF))
    a = jnp.transpose(a, tuple(range(nl)) + (nl + 1, nl, nl + 2))
    return a.reshape(lead + (SWA_KV,))


def _swa_o_cols(a):
    lead = a.shape[:-1]
    nl = len(lead)
    a = a.reshape(lead + (SWA_HKV, SWA_G, SWA_HD))
    a = jnp.transpose(a, tuple(range(nl)) + (nl + 1, nl, nl + 2))
    return a.reshape(lead + (SWA_Q,))


def _mid_columns(raw):
    lead = raw.shape[:-1]
    seg = lambda off, n: lax.slice_in_dim(raw, off - _R_GLR, off - _R_GLR + n, axis=raw.ndim - 1)
    mid = jnp.concatenate([
        seg(_R_GLR, GLA_RANK),
        jnp.zeros(lead + (GLR_PAD - GLA_RANK,), raw.dtype),
        _swa_q_cols(seg(_R_SQ, SWA_Q)),
        _swa_k_cols(seg(_R_SK, SWA_KV)),
        seg(_R_SV, SWA_KV),
        _swa_o_cols(seg(_R_SZ, SWA_Q)),
    ], axis=-1)
    assert mid.shape[-1] == C_XQ - C_GLR
    return mid


def _kernel_columns(a):
    seg = lambda off, n: lax.slice_in_dim(a, off, off + n, axis=a.ndim - 1)
    return seg(_R_GQ, C_GLR), _mid_columns(seg(_R_GLR, _R_XQ - _R_GLR)), seg(_R_XQ, C_TOTAL - C_XQ)


def _w_in_cast_kernel(w_ref, lo_ref, mid_ref, hi_ref):
    lo_ref[...] = w_ref[0, :, _R_GQ:_R_GLR].astype(jnp.bfloat16)
    mid_ref[...] = w_ref[0, :, _R_GLR:_R_XQ].astype(jnp.bfloat16)
    hi_ref[...] = w_ref[0, :, _R_XQ:].astype(jnp.bfloat16)


def _w_in_cast(w_in, layer):
    _, D, n_cols = w_in.shape
    widths = (_R_GLR - _R_GQ, _R_XQ - _R_GLR, n_cols - _R_XQ)
    return pl.pallas_call(
        _w_in_cast_kernel,
        grid=(D // CAST_ROWS,),
        in_specs=[pl.BlockSpec((1, CAST_ROWS, n_cols), lambda i: (layer, i, 0))],
        out_specs=[pl.BlockSpec((CAST_ROWS, w), lambda i: (i, 0)) for w in widths],
        out_shape=[jax.ShapeDtypeStruct((D, w), jnp.bfloat16) for w in widths],
        compiler_params=pltpu.CompilerParams(dimension_semantics=("arbitrary",)),
        name="w_in_cast",
    )(w_in)


def _dot(a, b):
    return jnp.dot(a, b, preferred_element_type=jnp.float32)


def _dot_nt(a, b):
    return lax.dot_general(a, b, (((1,), (1,)), ((), ())), preferred_element_type=jnp.float32)


def _dot_tn(a, b):
    return lax.dot_general(a, b, (((0,), (0,)), ((), ())), preferred_element_type=jnp.float32)


def _sigmoid(x):
    return 1.0 / (1.0 + jnp.exp(-x))


def _silu(x):
    return x * _sigmoid(x)


def _log_sigmoid(x):
    return jnp.minimum(x, 0.0) - jnp.log(1.0 + jnp.exp(-jnp.abs(x)))


def _split_bf16(x):
    hi = x.astype(jnp.bfloat16)
    lo = (x - hi.astype(jnp.float32)).astype(jnp.bfloat16)
    return hi, lo


def _mem_kv_kernel(mem_ref, g_ref, w_ref, out_ref):
    m = mem_ref[0]
    y = m * lax.rsqrt(jnp.mean(m * m, axis=-1, keepdims=True) + EPS) * g_ref[...]
    out_ref[0] = _dot(y.astype(jnp.bfloat16), w_ref[...]).astype(jnp.bfloat16)


def _mem_kv(mem, g_mem, w_mem_kv_bf16):
    B, M, D = mem.shape
    N = w_mem_kv_bf16.shape[1]
    return pl.pallas_call(
        _mem_kv_kernel,
        grid=(B,),
        in_specs=[
            pl.BlockSpec((1, M, D), lambda b: (b, 0, 0)),
            pl.BlockSpec((1, D), lambda b: (0, 0)),
            pl.BlockSpec((D, N), lambda b: (0, 0)),
        ],
        out_specs=pl.BlockSpec((1, M, N), lambda b: (b, 0, 0)),
        out_shape=jax.ShapeDtypeStruct((B, M, N), jnp.bfloat16),
        compiler_params=pltpu.CompilerParams(dimension_semantics=("arbitrary",)),
        name="mem_kv",
    )(mem, g_mem.reshape(1, D), w_mem_kv_bf16)


def _layer_kernel(sinks_ref, x_ref, mkv_ref, cos_ref, sin_ref, w_lo_ref, w_mid_ref, w_hi_ref, b_in_ref,
                  w_up_ref, b_gate_ref, g_mix_ref, g_gla_ref, w_br_gla_ref, w_br_swa_ref, w_br_mem_ref,
                  w_out_ref, g_final_ref, tri_ref, bias_ref,
                  out_ref,
                  st_ref, kprev_ref, vprev_ref, ya_ref, yb_ref, yc_ref,
                  *, seq_tile, apply_final_norm):
    T = seq_tile
    f32 = jnp.float32
    bf16 = jnp.bfloat16
    s_idx = pl.program_id(1)

    rd = s_idx % 2
    wr = 1 - rd

    @pl.when(s_idx == 0)
    def _():
        st_ref[0] = jnp.zeros(st_ref.shape[1:], st_ref.dtype)
        kprev_ref[0] = jnp.zeros(kprev_ref.shape[1:], kprev_ref.dtype)
        vprev_ref[0] = jnp.zeros(vprev_ref.shape[1:], vprev_ref.dtype)

    x = x_ref[0]
    hb_parts = []
    for i in range(T // NORM_ROWS):
        xi = x_ref[0, i * NORM_ROWS:(i + 1) * NORM_ROWS]
        hb_parts.append((xi * lax.rsqrt(jnp.mean(xi * xi, axis=-1, keepdims=True) + EPS)
                         * g_mix_ref[...]).astype(bf16))
    hb = jnp.concatenate(hb_parts, axis=0)

    def proj(lo, width):
        if lo < C_GLR:
            w = w_lo_ref[:, lo:lo + width]
        elif lo < C_XQ:
            w = w_mid_ref[:, lo - C_GLR:lo - C_GLR + width]
        else:
            w = w_hi_ref[:, lo - C_XQ:lo - C_XQ + width]
        return _dot(hb, w) + b_in_ref[:, lo:lo + width]

    n_slice = D_MODEL // MXU_N
    late = {}
    late_specs = {"gz": (C_GZ, None), "sz": (C_SZ, _silu), "xq": (C_XQ, None), "xz": (C_XZ, _silu),
                  "g0": (C_GATES, _sigmoid), "g1": (C_GATES + D_MODEL, _sigmoid),
                  "g2": (C_GATES + 2 * D_MODEL, _sigmoid)}
    pending = [(name, i) for name in FILL_ORDER for i in range(n_slice)]

    def fill(count):
        for _ in range(min(count, len(pending))):
            name, i = pending.pop(0)
            lo, act = late_specs[name]
            v = proj(lo + i * MXU_N, MXU_N)
            late[name, i] = v if act is None else act(v)

    def need(name):
        while any((name, i) not in late for i in range(n_slice)):
            fill(1)

    def late_full(name):
        need(name)
        return jnp.concatenate([late[name, i] for i in range(n_slice)], axis=1)

    gq = jnp.concatenate([_dot(hp, w_lo_ref[:, C_GQ:C_GQ + GLA_QK]) for hp in hb_parts],
                         axis=0) + b_in_ref[:, C_GQ:C_GQ + GLA_QK]
    gk = proj(C_GK, GLA_QK)
    gvb = proj(C_GV, GLA_V).astype(bf16)
    glr = proj(C_GLR, GLR_PAD)
    pre = _dot(glr.astype(bf16), w_up_ref[...]) + b_gate_ref[...]
    la = _log_sigmoid(pre) * (1.0 / GLA_TAU)
    la_hi, la_lo = _split_bf16(la)

    sq = proj(C_SQ, SWA_Q)
    sk = proj(C_SK, SWA_KV)
    vb = proj(C_SV, SWA_KV).astype(bf16)
    cs = cos_ref[...]
    sn = sin_ref[...]
    qscale = SWA_HD ** -0.5
    half = SWA_Q // 2
    q_cols = []
    for c in range(SWA_G):
        q1 = sq[:, c * LANES:(c + 1) * LANES]
        q2 = sq[:, half + c * LANES:half + (c + 1) * LANES]
        q_cols.append(jnp.concatenate([((q1 * cs - q2 * sn) * qscale).astype(bf16),
                                       ((q2 * cs + q1 * sn) * qscale).astype(bf16)], axis=1))
    k1 = sk[:, :LANES]
    k2 = sk[:, LANES:]
    kr = jnp.concatenate([k1 * cs - k2 * sn, k2 * cs + k1 * sn], axis=1).astype(bf16)
    kprev_ref[wr] = kr[T - WINDOW:T]
    vprev_ref[wr] = vb[T - WINDOW:T]

    tri = tri_ref[...]
    tri_mask = tri.astype(f32) > 0.5
    g_gla = g_gla_ref[...]
    n_chunk = GLA_TILE // GLA_CHUNK

    def chunk_columns(a):
        cols = []
        for c in range(n_chunk):
            parts = []
            if c > 0:
                parts.append(jnp.zeros((c * GLA_CHUNK, GLA_DK), a.dtype))
            parts.append(a[c * GLA_CHUNK:(c + 1) * GLA_CHUNK])
            if c < n_chunk - 1:
                parts.append(jnp.zeros(((n_chunk - 1 - c) * GLA_CHUNK, GLA_DK), a.dtype))
            cols.append(jnp.concatenate(parts, axis=0))
        return jnp.concatenate(cols, axis=1)

    state = [st_ref[rd, hh] for hh in range(GLA_H)]
    for g in range(T // GLA_TILE):
        r0 = g * GLA_TILE
        lah = la_hi[r0:r0 + GLA_TILE]
        lal = la_lo[r0:r0 + GLA_TILE]
        b = _dot(tri, lah) + _dot(tri, lal)
        bl = jnp.concatenate(
            [jnp.broadcast_to(b[(c + 1) * GLA_CHUNK - 1:(c + 1) * GLA_CHUNK], (GLA_CHUNK, GLA_QK))
             for c in range(n_chunk)], axis=0)
        gq_g = gq[r0:r0 + GLA_TILE]
        gk_g = gk[r0:r0 + GLA_TILE]
        qd = (gq_g * (GLA_DK ** -0.5) * jnp.exp(b)).astype(bf16)
        kn = (gk_g * jnp.exp(-b)).astype(bf16)
        kd = (gk_g * jnp.exp(bl - b)).astype(bf16)
        dec = jnp.exp(bl)
        fill(FILL_PER_GLA_TILE)
        need("gz")
        for hh in range(GLA_H):
            k0 = hh * GLA_DK
            v0 = hh * GLA_DV
            qd_h = qd[:, k0:k0 + GLA_DK]
            v_h = gvb[r0:r0 + GLA_TILE, v0:v0 + GLA_DV]
            att = jnp.where(tri_mask, _dot_nt(qd_h, kn[:, k0:k0 + GLA_DK]), 0.0).astype(bf16)
            o = _dot(att, v_h)
            upd = _dot_tn(v_h, chunk_columns(kd[:, k0:k0 + GLA_DK]))
            st = state[hh]
            sts = []
            for c in range(n_chunk):
                sts.append(st.astype(bf16))
                st = (dec[c * GLA_CHUNK:c * GLA_CHUNK + 1, k0:k0 + GLA_DK] * st
                      + upd[:, c * GLA_DK:(c + 1) * GLA_DK])
            state[hh] = st
            o = o + _dot_nt(chunk_columns(qd_h), jnp.concatenate(sts, axis=1))
            on = o * lax.rsqrt(jnp.mean(o * o, axis=-1, keepdims=True) + EPS) * g_gla
            ya_ref[r0:r0 + GLA_TILE, v0:v0 + GLA_DV] = (
                on * _silu(late["gz", hh][r0:r0 + GLA_TILE])).astype(bf16)
    for hh in range(GLA_H):
        st_ref[wr, hh] = state[hh]
    need("sz")
    merged = late_full("g0") * _dot(ya_ref[...], w_br_gla_ref[...])

    klane = lax.broadcasted_iota(jnp.int32, (1, SWA_KV), 1)
    kgrp = (klane % LANES) // SWA_HALF
    vgrp = klane // SWA_HD
    zero_b = jnp.zeros((), bf16)
    for n in range(T // WINDOW):
        r0 = n * WINDOW
        if n == 0:
            kp = kprev_ref[rd]
            vp = vprev_ref[rd]
            bias = bias_ref[jnp.where(s_idx == 0, 1, 0)]
        else:
            kp = kr[r0 - WINDOW:r0]
            vp = vb[r0 - WINDOW:r0]
            bias = bias_ref[0]
        kcat = jnp.concatenate([kp, kr[r0:r0 + WINDOW]], axis=0)
        vcat = jnp.concatenate([vp, vb[r0:r0 + WINDOW]], axis=0)
        rhs = jnp.concatenate([jnp.where(kgrp == j, kcat, zero_b) for j in range(SWA_HKV)], axis=0)
        vstk = jnp.concatenate([jnp.where(vgrp == j, vcat, zero_b) for j in range(SWA_HKV)], axis=0)
        lhs = jnp.concatenate([q_cols[c][r0:r0 + WINDOW] for c in range(SWA_G)], axis=0)
        s = _dot_nt(lhs, rhs)
        for c in range(SWA_G):
            ps = []
            for j in range(SWA_HKV):
                sc = s[c * WINDOW:(c + 1) * WINDOW, j * 2 * WINDOW:(j + 1) * 2 * WINDOW] + bias
                sink = sinks_ref[j * SWA_G + c]
                m = jnp.maximum(jnp.max(sc, axis=-1, keepdims=True), sink)
                p = jnp.exp(sc - m)
                l = jnp.sum(p, axis=-1, keepdims=True) + jnp.exp(sink - m)
                ps.append((p * (1.0 / l)).astype(bf16))
                if (c * SWA_HKV + j) % FILL_SOFTMAX_PERIOD == FILL_SOFTMAX_PERIOD - 1:
                    fill(1)
            o = _dot(jnp.concatenate(ps, axis=1), vstk)
            yb_ref[r0:r0 + WINDOW, c * SWA_KV:(c + 1) * SWA_KV] = (
                o * late["sz", c][r0:r0 + WINDOW]).astype(bf16)
    merged = merged + late_full("g1") * _dot(yb_ref[...], w_br_swa_ref[...])

    need("xq")
    need("xz")
    for hh in range(X_H):
        c0 = hh * X_HD
        qh = (late["xq", hh] * (X_HD ** -0.5)).astype(bf16)
        sc = _dot_nt(qh, mkv_ref[0, :, c0:c0 + X_HD])
        m = jnp.max(sc, axis=-1, keepdims=True)
        p = jnp.exp(sc - m)
        l = jnp.sum(p, axis=-1, keepdims=True)
        o = _dot((p * (1.0 / l)).astype(bf16), mkv_ref[0, :, X_W + c0:X_W + c0 + X_HD])
        yc_ref[:, c0:c0 + X_HD] = (o * late["xz", hh]).astype(bf16)
    merged = merged + late_full("g2") * _dot(yc_ref[...], w_br_mem_ref[...])

    mb = merged.astype(bf16)
    for i in range(T // OUT_ROWS):
        rows = slice(i * OUT_ROWS, (i + 1) * OUT_ROWS)
        xo = x_ref[0, rows] + _dot(mb[rows], w_out_ref[...])
        if apply_final_norm:
            xo = xo * lax.rsqrt(jnp.mean(xo * xo, axis=-1, keepdims=True) + EPS) * g_final_ref[...]
        out_ref[0, rows] = xo


def _seq_tile(S):
    for t in (512, 256, 128):
        if S % t == 0:
            return t
    raise ValueError(f"sequence length {S} must be a multiple of {WINDOW}")


def _layer(x, mkv, cos_t, sin_t, w_in_parts, b_in, w_up, b_gate, g_mix, g_gla, sinks, w_br_gla, w_br_swa,
           w_br_mem, w_out, g_final, apply_final_norm):
    B, S, D = x.shape
    T = _seq_tile(S)
    M = mkv.shape[1]
    n_tiles = S // T

    rows = np.arange(GLA_TILE)
    same_chunk = (rows[:, None] // GLA_CHUNK) == (rows[None, :] // GLA_CHUNK)
    tri = jnp.asarray(same_chunk & (rows[None, :] <= rows[:, None]), jnp.bfloat16)
    qi = np.arange(WINDOW)[:, None]
    kj = np.arange(2 * WINDOW)[None, :]
    band = (kj <= qi + WINDOW) & (kj > qi)
    bias_np = np.stack([np.where(band, 0.0, -np.inf),
                        np.where(band & (kj >= WINDOW), 0.0, -np.inf)]).astype(np.float32)
    bias = jnp.asarray(bias_np)

    def const(shape):
        nd = len(shape)
        return pl.BlockSpec(shape, lambda b, s: (0,) * nd, pipeline_mode=pl.Buffered(1))

    in_specs = [
        pl.BlockSpec(memory_space=pltpu.SMEM),
        pl.BlockSpec((1, T, D), lambda b, s: (b, s, 0)),
        pl.BlockSpec((1, M, 2 * X_W), lambda b, s: (b, 0, 0)),
        pl.BlockSpec((T, LANES), lambda b, s: (s, 0)),
        pl.BlockSpec((T, LANES), lambda b, s: (s, 0)),
        const((D, C_GLR)),
        const((D, C_XQ - C_GLR)),
        const((D, C_TOTAL - C_XQ)),
        const((1, C_TOTAL)),
        const((GLR_PAD, GLA_QK)),
        const((1, GLA_QK)),
        const((1, D)),
        const((1, GLA_DV)),
        const((GLA_V, D)),
        const((SWA_Q, D)),
        const((X_W, D)),
        const((D, D)),
        const((1, D)),
        const((GLA_TILE, GLA_TILE)),
        const((2, WINDOW, 2 * WINDOW)),
    ]
    scratch = [
        pltpu.VMEM((2, GLA_H, GLA_DV, GLA_DK), jnp.float32),
        pltpu.VMEM((2, WINDOW, SWA_KV), jnp.bfloat16),
        pltpu.VMEM((2, WINDOW, SWA_KV), jnp.bfloat16),
        pltpu.VMEM((T, GLA_V), jnp.bfloat16),
        pltpu.VMEM((T, SWA_Q), jnp.bfloat16),
        pltpu.VMEM((T, X_W), jnp.bfloat16),
    ]
    kern = functools.partial(_layer_kernel, seq_tile=T, apply_final_norm=apply_final_norm)
    return pl.pallas_call(
        kern,
        grid=(B, n_tiles),
        in_specs=in_specs,
        out_specs=pl.BlockSpec((1, T, D), lambda b, s: (b, s, 0)),
        out_shape=jax.ShapeDtypeStruct((B, S, D), jnp.float32),
        scratch_shapes=scratch,
        compiler_params=pltpu.CompilerParams(
            dimension_semantics=("arbitrary", "arbitrary"),
            vmem_limit_bytes=60 * 1024 * 1024),
        name="layer",
    )(sinks, x, mkv, cos_t, sin_t, *w_in_parts, b_in, w_up, b_gate, g_mix, g_gla, w_br_gla, w_br_swa,
      w_br_mem, w_out, g_final, tri, bias)


def kernel(x, mem, g_mix, g_mem, w_in, b_in, w_gla_gate_up, b_gla_gate, g_gla_norm, sinks, w_mem_kv,
           w_br_gla, w_br_swa, w_br_mem, w_out, g_final):
    B, S, D = x.shape
    depth = w_in.shape[0]
    bf16 = jnp.bfloat16

    inv = ROPE_THETA ** (-jnp.arange(SWA_HALF, dtype=jnp.float32) / SWA_HALF)
    ang = jnp.arange(S).astype(jnp.float32)[:, None] * inv[None, :]
    reps = LANES // SWA_HALF
    cos_t = jnp.tile(jnp.cos(ang), (1, reps))
    sin_t = jnp.tile(jnp.sin(ang), (1, reps))

    for l in range(depth):
        w_lo, w_mid_raw, w_hi = _w_in_cast(w_in, l)
        w_in_k = (w_lo, _mid_columns(w_mid_raw), w_hi)
        b_in_k = jnp.concatenate(_kernel_columns(b_in[l])).reshape(1, C_TOTAL)
        w_up = jnp.zeros((GLR_PAD, GLA_QK), bf16).at[:GLA_RANK].set(w_gla_gate_up[l].astype(bf16))
        w_br_swa_k = jnp.transpose(
            w_br_swa[l].astype(bf16).reshape(SWA_HKV, SWA_G, SWA_HD, D), (1, 0, 2, 3)).reshape(SWA_Q, D)
        mkv = _mem_kv(mem, g_mem[l], w_mem_kv[l].astype(bf16))
        x = _layer(
            x, mkv, cos_t, sin_t, w_in_k, b_in_k, w_up, b_gla_gate[l].reshape(1, GLA_QK),
            g_mix[l].reshape(1, D), g_gla_norm[l].reshape(1, GLA_DV), sinks[l],
            w_br_gla[l].astype(bf16), w_br_swa_k,
            w_br_mem[l].astype(bf16), w_out[l].astype(bf16), g_final.reshape(1, D),
            apply_final_norm=(l == depth - 1))
    return x
```

```python
import functools

import numpy as np
import jax
import jax.numpy as jnp
from jax import lax
from jax.experimental import pallas as pl
from jax.experimental.pallas import tpu as pltpu

D_MODEL = 1024
MEM_LEN = 256
GLA_H = 4
GLA_QK = 512
GLA_V = 1024
GLA_DK = 128
GLA_DV = 256
GLA_RANK = 16
GLA_TAU = 16.0
GLA_CHUNK = 64
GLA_TILE = 256
SWA_HD = 64
SWA_HALF = SWA_HD // 2
SWA_HQ = 16
SWA_HKV = 4
SWA_G = SWA_HQ // SWA_HKV
SWA_Q = 1024
SWA_KV = 256
WINDOW = 128
ROPE_THETA = 10000.0
X_H = 4
X_HD = 256
X_W = 1024
N_BRANCH = 3
EPS = 1e-6

LANES = 128
MXU_N = 256

_IN_SIZES = (GLA_QK, GLA_QK, GLA_V, GLA_V, GLA_RANK, SWA_Q, SWA_KV, SWA_KV, SWA_Q, X_W, X_W,
             N_BRANCH * D_MODEL)
_IN_OFF = np.concatenate([[0], np.cumsum(_IN_SIZES)]).astype(np.int64)
(_R_GQ, _R_GK, _R_GV, _R_GZ, _R_GLR, _R_SQ, _R_SK, _R_SV, _R_SZ, _R_XQ, _R_XZ, _R_GATES) = (
    int(o) for o in _IN_OFF[:-1])

GLR_PAD = LANES
C_GQ = 0
C_GK = C_GQ + GLA_QK
C_GV = C_GK + GLA_QK
C_GZ = C_GV + GLA_V
C_GLR = C_GZ + GLA_V
C_SQ = C_GLR + GLR_PAD
C_SK = C_SQ + SWA_Q
C_SV = C_SK + SWA_KV
C_SZ = C_SV + SWA_KV
C_XQ = C_SZ + SWA_Q
C_XZ = C_XQ + X_W
C_GATES = C_XZ + X_W
C_TOTAL = C_GATES + N_BRANCH * D_MODEL

FILL_ORDER = ("gz", "sz", "g0", "xq", "xz", "g1", "g2")
FILL_PER_GLA_TILE = 4
FILL_SOFTMAX_PERIOD = 4
NORM_ROWS = 128
OUT_ROWS = 256


def _swa_q_cols(a):
    lead = a.shape[:-1]
    nl = len(lead)
    a = a.reshape(lead + (SWA_HKV, SWA_G, 2, SWA_HALF))
    a = jnp.transpose(a, tuple(range(nl)) + (nl + 2, nl + 1, nl, nl + 3))
    return a.reshape(lead + (SWA_Q,))


def _swa_k_cols(a):
    lead = a.shape[:-1]
    nl = len(lead)
    a = a.reshape(lead + (SWA_HKV, 2, SWA_HALF))
    a = jnp.transpose(a, tuple(range(nl)) + (nl + 1, nl, nl + 2))
    return a.reshape(lead + (SWA_KV,))


def _swa_o_cols(a):
    lead = a.shape[:-1]
    nl = len(lead)
    a = a.reshape(lead + (SWA_HKV, SWA_G, SWA_HD))
    a = jnp.transpose(a, tuple(range(nl)) + (nl + 1, nl, nl + 2))
    return a.reshape(lead + (SWA_Q,))


def _mid_columns(raw):
    lead = raw.shape[:-1]
    seg = lambda off, n: lax.slice_in_dim(raw, off - _R_GLR, off - _R_GLR + n, axis=raw.ndim - 1)
    mid = jnp.concatenate([
        seg(_R_GLR, GLA_RANK),
        jnp.zeros(lead + (GLR_PAD - GLA_RANK,), raw.dtype),
        _swa_q_cols(seg(_R_SQ, SWA_Q)),
        _swa_k_cols(seg(_R_SK, SWA_KV)),
        seg(_R_SV, SWA_KV),
        _swa_o_cols(seg(_R_SZ, SWA_Q)),
    ], axis=-1)
    assert mid.shape[-1] == C_XQ - C_GLR
    return mid


def _kernel_columns(a):
    seg = lambda off, n: lax.slice_in_dim(a, off, off + n, axis=a.ndim - 1)
    return seg(_R_GQ, C_GLR), _mid_columns(seg(_R_GLR, _R_XQ - _R_GLR)), seg(_R_XQ, C_TOTAL - C_XQ)


def _column_block_sources():
    run = lambda off, n: [[(off + i * LANES, LANES)] for i in range(n // LANES)]
    blocks = run(_R_GQ, C_GLR)
    blocks.append([(_R_GLR, GLA_RANK), (None, GLR_PAD - GLA_RANK)])
    for t in range(2):
        for c in range(SWA_G):
            blocks.append([(_R_SQ + (j * SWA_G + c) * SWA_HD + t * SWA_HALF, SWA_HALF)
                           for j in range(SWA_HKV)])
    for t in range(2):
        blocks.append([(_R_SK + j * SWA_HD + t * SWA_HALF, SWA_HALF) for j in range(SWA_HKV)])
    blocks += run(_R_SV, SWA_KV)
    for c in range(SWA_G):
        for jp in range(SWA_HKV // 2):
            blocks.append([(_R_SZ + (j * SWA_G + c) * SWA_HD, SWA_HD) for j in (2 * jp, 2 * jp + 1)])
    blocks += run(_R_XQ, C_TOTAL - C_XQ)
    assert len(blocks) * LANES == C_TOTAL
    return blocks


def _w_in_cast_kernel(wt_ref, out_ref):
    for d, runs in enumerate(_column_block_sources()):
        parts = [jnp.zeros((n, LANES), jnp.float32) if src is None else wt_ref[src:src + n, :]
                 for src, n in runs]
        blk = parts[0] if len(parts) == 1 else jnp.concatenate(parts, axis=0)
        out_ref[:, d * LANES:(d + 1) * LANES] = blk.T.astype(jnp.bfloat16)


def _w_in_cast(w_t):
    n_cols, D = w_t.shape
    return pl.pallas_call(
        _w_in_cast_kernel,
        grid=(D // LANES,),
        in_specs=[pl.BlockSpec((n_cols, LANES), lambda i: (0, i))],
        out_specs=pl.BlockSpec((LANES, C_TOTAL), lambda i: (i, 0)),
        out_shape=jax.ShapeDtypeStruct((D, C_TOTAL), jnp.bfloat16),
        compiler_params=pltpu.CompilerParams(dimension_semantics=("arbitrary",)),
        name="w_in_cast",
    )(w_t)


def _dot(a, b):
    return jnp.dot(a, b, preferred_element_type=jnp.float32)


def _dot_nt(a, b):
    return lax.dot_general(a, b, (((1,), (1,)), ((), ())), preferred_element_type=jnp.float32)


def _dot_tn(a, b):
    return lax.dot_general(a, b, (((0,), (0,)), ((), ())), preferred_element_type=jnp.float32)


def _sigmoid(x):
    return 1.0 / (1.0 + jnp.exp(-x))


def _silu(x):
    return x * _sigmoid(x)


def _log_sigmoid(x):
    return jnp.minimum(x, 0.0) - jnp.log(1.0 + jnp.exp(-jnp.abs(x)))


def _split_bf16(x):
    hi = x.astype(jnp.bfloat16)
    lo = (x - hi.astype(jnp.float32)).astype(jnp.bfloat16)
    return hi, lo


def _mem_kv_kernel(mem_ref, g_ref, w_ref, out_ref):
    m = mem_ref[0]
    y = m * lax.rsqrt(jnp.mean(m * m, axis=-1, keepdims=True) + EPS) * g_ref[...]
    out_ref[0] = _dot(y.astype(jnp.bfloat16), w_ref[...]).astype(jnp.bfloat16)


def _mem_kv(mem, g_mem, w_mem_kv_bf16):
    B, M, D = mem.shape
    N = w_mem_kv_bf16.shape[1]
    return pl.pallas_call(
        _mem_kv_kernel,
        grid=(B,),
        in_specs=[
            pl.BlockSpec((1, M, D), lambda b: (b, 0, 0)),
            pl.BlockSpec((1, D), lambda b: (0, 0)),
            pl.BlockSpec((D, N), lambda b: (0, 0)),
        ],
        out_specs=pl.BlockSpec((1, M, N), lambda b: (b, 0, 0)),
        out_shape=jax.ShapeDtypeStruct((B, M, N), jnp.bfloat16),
        compiler_params=pltpu.CompilerParams(dimension_semantics=("arbitrary",)),
        name="mem_kv",
    )(mem, g_mem.reshape(1, D), w_mem_kv_bf16)


def _layer_kernel(sinks_ref, x_ref, mkv_ref, cos_ref, sin_ref, w_in_ref, b_in_ref,
                  w_up_ref, b_gate_ref, g_mix_ref, g_gla_ref, w_br_gla_ref, w_br_swa_ref, w_br_mem_ref,
                  w_out_ref, g_final_ref, tri_ref, bias_ref,
                  out_ref,
                  st_ref, kprev_ref, vprev_ref, ya_ref, yb_ref, yc_ref,
                  *, seq_tile, apply_final_norm):
    T = seq_tile
    f32 = jnp.float32
    bf16 = jnp.bfloat16
    s_idx = pl.program_id(1)

    rd = s_idx % 2
    wr = 1 - rd

    @pl.when(s_idx == 0)
    def _():
        st_ref[0] = jnp.zeros(st_ref.shape[1:], st_ref.dtype)
        kprev_ref[0] = jnp.zeros(kprev_ref.shape[1:], kprev_ref.dtype)
        vprev_ref[0] = jnp.zeros(vprev_ref.shape[1:], vprev_ref.dtype)

    x = x_ref[0]
    hb_parts = []
    for i in range(T // NORM_ROWS):
        xi = x_ref[0, i * NORM_ROWS:(i + 1) * NORM_ROWS]
        hb_parts.append((xi * lax.rsqrt(jnp.mean(xi * xi, axis=-1, keepdims=True) + EPS)
                         * g_mix_ref[...]).astype(bf16))
    hb = jnp.concatenate(hb_parts, axis=0)

    def proj(lo, width):
        return _dot(hb, w_in_ref[:, lo:lo + width]) + b_in_ref[:, lo:lo + width]

    n_slice = D_MODEL // MXU_N
    late = {}
    late_specs = {"gz": (C_GZ, None), "sz": (C_SZ, _silu), "xq": (C_XQ, None), "xz": (C_XZ, _silu),
                  "g0": (C_GATES, _sigmoid), "g1": (C_GATES + D_MODEL, _sigmoid),
                  "g2": (C_GATES + 2 * D_MODEL, _sigmoid)}
    pending = [(name, i) for name in FILL_ORDER for i in range(n_slice)]

    def fill(count):
        for _ in range(min(count, len(pending))):
            name, i = pending.pop(0)
            lo, act = late_specs[name]
            v = proj(lo + i * MXU_N, MXU_N)
            late[name, i] = v if act is None else act(v)

    def need(name):
        while any((name, i) not in late for i in range(n_slice)):
            fill(1)

    def late_full(name):
        need(name)
        return jnp.concatenate([late[name, i] for i in range(n_slice)], axis=1)

    gq = jnp.concatenate([_dot(hp, w_in_ref[:, C_GQ:C_GQ + GLA_QK]) for hp in hb_parts],
                         axis=0) + b_in_ref[:, C_GQ:C_GQ + GLA_QK]
    gk = proj(C_GK, GLA_QK)
    gvb = proj(C_GV, GLA_V).astype(bf16)
    glr = proj(C_GLR, GLR_PAD)
    pre = _dot(glr.astype(bf16), w_up_ref[...]) + b_gate_ref[...]
    la = _log_sigmoid(pre) * (1.0 / GLA_TAU)
    la_hi, la_lo = _split_bf16(la)

    sq = proj(C_SQ, SWA_Q)
    sk = proj(C_SK, SWA_KV)
    vb = proj(C_SV, SWA_KV).astype(bf16)
    cs = cos_ref[...]
    sn = sin_ref[...]
    qscale = SWA_HD ** -0.5
    half = SWA_Q // 2
    q_cols = []
    for c in range(SWA_G):
        q1 = sq[:, c * LANES:(c + 1) * LANES]
        q2 = sq[:, half + c * LANES:half + (c + 1) * LANES]
        q_cols.append(jnp.concatenate([((q1 * cs - q2 * sn) * qscale).astype(bf16),
                                       ((q2 * cs + q1 * sn) * qscale).astype(bf16)], axis=1))
    k1 = sk[:, :LANES]
    k2 = sk[:, LANES:]
    kr = jnp.concatenate([k1 * cs - k2 * sn, k2 * cs + k1 * sn], axis=1).astype(bf16)
    kprev_ref[wr] = kr[T - WINDOW:T]
    vprev_ref[wr] = vb[T - WINDOW:T]

    tri = tri_ref[...]
    tri_mask = tri.astype(f32) > 0.5
    g_gla = g_gla_ref[...]
    n_chunk = GLA_TILE // GLA_CHUNK

    def chunk_columns(a):
        cols = []
        for c in range(n_chunk):
            parts = []
            if c > 0:
                parts.append(jnp.zeros((c * GLA_CHUNK, GLA_DK), a.dtype))
            parts.append(a[c * GLA_CHUNK:(c + 1) * GLA_CHUNK])
            if c < n_chunk - 1:
                parts.append(jnp.zeros(((n_chunk - 1 - c) * GLA_CHUNK, GLA_DK), a.dtype))
            cols.append(jnp.concatenate(parts, axis=0))
        return jnp.concatenate(cols, axis=1)

    state = [st_ref[rd, hh] for hh in range(GLA_H)]
    for g in range(T // GLA_TILE):
        r0 = g * GLA_TILE
        lah = la_hi[r0:r0 + GLA_TILE]
        lal = la_lo[r0:r0 + GLA_TILE]
        b = _dot(tri, lah) + _dot(tri, lal)
        bl = jnp.concatenate(
            [jnp.broadcast_to(b[(c + 1) * GLA_CHUNK - 1:(c + 1) * GLA_CHUNK], (GLA_CHUNK, GLA_QK))
             for c in range(n_chunk)], axis=0)
        gq_g = gq[r0:r0 + GLA_TILE]
        gk_g = gk[r0:r0 + GLA_TILE]
        qd = (gq_g * (GLA_DK ** -0.5) * jnp.exp(b)).astype(bf16)
        kn = (gk_g * jnp.exp(-b)).astype(bf16)
        kd = (gk_g * jnp.exp(bl - b)).astype(bf16)
        dec = jnp.exp(bl)
        fill(FILL_PER_GLA_TILE)
        need("gz")
        for hh in range(GLA_H):
            k0 = hh * GLA_DK
            v0 = hh * GLA_DV
            qd_h = qd[:, k0:k0 + GLA_DK]
            v_h = gvb[r0:r0 + GLA_TILE, v0:v0 + GLA_DV]
            att = jnp.where(tri_mask, _dot_nt(qd_h, kn[:, k0:k0 + GLA_DK]), 0.0).astype(bf16)
            o = _dot(att, v_h)
            upd = _dot_tn(v_h, chunk_columns(kd[:, k0:k0 + GLA_DK]))
            st = state[hh]
            sts = []
            for c in range(n_chunk):
                sts.append(st.astype(bf16))
                st = (dec[c * GLA_CHUNK:c * GLA_CHUNK + 1, k0:k0 + GLA_DK] * st
                      + upd[:, c * GLA_DK:(c + 1) * GLA_DK])
            state[hh] = st
            o = o + _dot_nt(chunk_columns(qd_h), jnp.concatenate(sts, axis=1))
            on = o * lax.rsqrt(jnp.mean(o * o, axis=-1, keepdims=True) + EPS) * g_gla
            ya_ref[r0:r0 + GLA_TILE, v0:v0 + GLA_DV] = (
                on * _silu(late["gz", hh][r0:r0 + GLA_TILE])).astype(bf16)
    for hh in range(GLA_H):
        st_ref[wr, hh] = state[hh]
    need("sz")
    merged = late_full("g0") * _dot(ya_ref[...], w_br_gla_ref[...])

    klane = lax.broadcasted_iota(jnp.int32, (1, SWA_KV), 1)
    kgrp = (klane % LANES) // SWA_HALF
    vgrp = klane // SWA_HD
    zero_b = jnp.zeros((), bf16)
    for n in range(T // WINDOW):
        r0 = n * WINDOW
        if n == 0:
            kp = kprev_ref[rd]
            vp = vprev_ref[rd]
            bias = bias_ref[jnp.where(s_idx == 0, 1, 0)]
        else:
            kp = kr[r0 - WINDOW:r0]
            vp = vb[r0 - WINDOW:r0]
            bias = bias_ref[0]
        kcat = jnp.concatenate([kp, kr[r0:r0 + WINDOW]], axis=0)
        vcat = jnp.concatenate([vp, vb[r0:r0 + WINDOW]], axis=0)
        rhs = jnp.concatenate([jnp.where(kgrp == j, kcat, zero_b) for j in range(SWA_HKV)], axis=0)
        vstk = jnp.concatenate([jnp.where(vgrp == j, vcat, zero_b) for j in range(SWA_HKV)], axis=0)
        lhs = jnp.concatenate([q_cols[c][r0:r0 + WINDOW] for c in range(SWA_G)], axis=0)
        s = _dot_nt(lhs, rhs)
        for c in range(SWA_G):
            ps = []
            for j in range(SWA_HKV):
                sc = s[c * WINDOW:(c + 1) * WINDOW, j * 2 * WINDOW:(j + 1) * 2 * WINDOW] + bias
                sink = sinks_ref[j * SWA_G + c]
                m = jnp.maximum(jnp.max(sc, axis=-1, keepdims=True), sink)
                p = jnp.exp(sc - m)
                l = jnp.sum(p, axis=-1, keepdims=True) + jnp.exp(sink - m)
                ps.append((p * (1.0 / l)).astype(bf16))
                if (c * SWA_HKV + j) % FILL_SOFTMAX_PERIOD == FILL_SOFTMAX_PERIOD - 1:
                    fill(1)
            o = _dot(jnp.concatenate(ps, axis=1), vstk)
            yb_ref[r0:r0 + WINDOW, c * SWA_KV:(c + 1) * SWA_KV] = (
                o * late["sz", c][r0:r0 + WINDOW]).astype(bf16)
    merged = merged + late_full("g1") * _dot(yb_ref[...], w_br_swa_ref[...])

    need("xq")
    need("xz")
    for hh in range(X_H):
        c0 = hh * X_HD
        qh = (late["xq", hh] * (X_HD ** -0.5)).astype(bf16)
        sc = _dot_nt(qh, mkv_ref[0, :, c0:c0 + X_HD])
        m = jnp.max(sc, axis=-1, keepdims=True)
        p = jnp.exp(sc - m)
        l = jnp.sum(p, axis=-1, keepdims=True)
        o = _dot((p * (1.0 / l)).astype(bf16), mkv_ref[0, :, X_W + c0:X_W + c0 + X_HD])
        yc_ref[:, c0:c0 + X_HD] = (o * late["xz", hh]).astype(bf16)
    merged = merged + late_full("g2") * _dot(yc_ref[...], w_br_mem_ref[...])

    mb = merged.astype(bf16)
    for i in range(T // OUT_ROWS):
        rows = slice(i * OUT_ROWS, (i + 1) * OUT_ROWS)
        xo = x_ref[0, rows] + _dot(mb[rows], w_out_ref[...])
        if apply_final_norm:
            xo = xo * lax.rsqrt(jnp.mean(xo * xo, axis=-1, keepdims=True) + EPS) * g_final_ref[...]
        out_ref[0, rows] = xo


def _seq_tile(S):
    for t in (512, 256, 128):
        if S % t == 0:
            return t
    raise ValueError(f"sequence length {S} must be a multiple of {WINDOW}")


def _layer(x, mkv, cos_t, sin_t, w_in, b_in, w_up, b_gate, g_mix, g_gla, sinks, w_br_gla, w_br_swa,
           w_br_mem, w_out, g_final, apply_final_norm):
    B, S, D = x.shape
    T = _seq_tile(S)
    M = mkv.shape[1]
    n_tiles = S // T

    rows = np.arange(GLA_TILE)
    same_chunk = (rows[:, None] // GLA_CHUNK) == (rows[None, :] // GLA_CHUNK)
    tri = jnp.asarray(same_chunk & (rows[None, :] <= rows[:, None]), jnp.bfloat16)
    qi = np.arange(WINDOW)[:, None]
    kj = np.arange(2 * WINDOW)[None, :]
    band = (kj <= qi + WINDOW) & (kj > qi)
    bias_np = np.stack([np.where(band, 0.0, -np.inf),
                        np.where(band & (kj >= WINDOW), 0.0, -np.inf)]).astype(np.float32)
    bias = jnp.asarray(bias_np)

    def const(shape):
        nd = len(shape)
        return pl.BlockSpec(shape, lambda b, s: (0,) * nd, pipeline_mode=pl.Buffered(1))

    in_specs = [
        pl.BlockSpec(memory_space=pltpu.SMEM),
        pl.BlockSpec((1, T, D), lambda b, s: (b, s, 0)),
        pl.BlockSpec((1, M, 2 * X_W), lambda b, s: (b, 0, 0)),
        pl.BlockSpec((T, LANES), lambda b, s: (s, 0)),
        pl.BlockSpec((T, LANES), lambda b, s: (s, 0)),
        const((D, C_TOTAL)),
        const((1, C_TOTAL)),
        const((GLR_PAD, GLA_QK)),
        const((1, GLA_QK)),
        const((1, D)),
        const((1, GLA_DV)),
        const((GLA_V, D)),
        const((SWA_Q, D)),
        const((X_W, D)),
        const((D, D)),
        const((1, D)),
        const((GLA_TILE, GLA_TILE)),
        const((2, WINDOW, 2 * WINDOW)),
    ]
    scratch = [
        pltpu.VMEM((2, GLA_H, GLA_DV, GLA_DK), jnp.float32),
        pltpu.VMEM((2, WINDOW, SWA_KV), jnp.bfloat16),
        pltpu.VMEM((2, WINDOW, SWA_KV), jnp.bfloat16),
        pltpu.VMEM((T, GLA_V), jnp.bfloat16),
        pltpu.VMEM((T, SWA_Q), jnp.bfloat16),
        pltpu.VMEM((T, X_W), jnp.bfloat16),
    ]
    kern = functools.partial(_layer_kernel, seq_tile=T, apply_final_norm=apply_final_norm)
    return pl.pallas_call(
        kern,
        grid=(B, n_tiles),
        in_specs=in_specs,
        out_specs=pl.BlockSpec((1, T, D), lambda b, s: (b, s, 0)),
        out_shape=jax.ShapeDtypeStruct((B, S, D), jnp.float32),
        scratch_shapes=scratch,
        compiler_params=pltpu.CompilerParams(
            dimension_semantics=("arbitrary", "arbitrary"),
            vmem_limit_bytes=60 * 1024 * 1024),
        name="layer",
    )(sinks, x, mkv, cos_t, sin_t, w_in, b_in, w_up, b_gate, g_mix, g_gla, w_br_gla, w_br_swa,
      w_br_mem, w_out, g_final, tri, bias)


def kernel(x, mem, g_mix, g_mem, w_in, b_in, w_gla_gate_up, b_gla_gate, g_gla_norm, sinks, w_mem_kv,
           w_br_gla, w_br_swa, w_br_mem, w_out, g_final):
    B, S, D = x.shape
    depth = w_in.shape[0]
    bf16 = jnp.bfloat16

    inv = ROPE_THETA ** (-jnp.arange(SWA_HALF, dtype=jnp.float32) / SWA_HALF)
    ang = jnp.arange(S).astype(jnp.float32)[:, None] * inv[None, :]
    reps = LANES // SWA_HALF
    cos_t = jnp.tile(jnp.cos(ang), (1, reps))
    sin_t = jnp.tile(jnp.sin(ang), (1, reps))

    for l in range(depth):
        w_in_k = _w_in_cast(jnp.transpose(w_in[l]))
        b_in_k = jnp.concatenate(_kernel_columns(b_in[l])).reshape(1, C_TOTAL)
        w_up = jnp.zeros((GLR_PAD, GLA_QK), bf16).at[:GLA_RANK].set(w_gla_gate_up[l].astype(bf16))
        w_br_swa_k = jnp.transpose(
            w_br_swa[l].astype(bf16).reshape(SWA_HKV, SWA_G, SWA_HD, D), (1, 0, 2, 3)).reshape(SWA_Q, D)
        mkv = _mem_kv(mem, g_mem[l], w_mem_kv[l].astype(bf16))
        x = _layer(
            x, mkv, cos_t, sin_t, w_in_k, b_in_k, w_up, b_gla_gate[l].reshape(1, GLA_QK),
            g_mix[l].reshape(1, D), g_gla_norm[l].reshape(1, GLA_DV), sinks[l],
            w_br_gla[l].astype(bf16), w_br_swa_k,
            w_br_mem[l].astype(bf16), w_out[l].astype(bf16), g_final.reshape(1, D),
            apply_final_norm=(l == depth - 1))
    return x
```

```python
import functools

import numpy as np
import jax
import jax.numpy as jnp
from jax import lax
from jax.experimental import pallas as pl
from jax.experimental.pallas import tpu as pltpu

D_MODEL = 1024
MEM_LEN = 256
GLA_H = 4
GLA_QK = 512
GLA_V = 1024
GLA_DK = 128
GLA_DV = 256
GLA_RANK = 16
GLA_TAU = 16.0
GLA_CHUNK = 64
GLA_TILE = 256
SWA_HD = 64
SWA_HALF = SWA_HD // 2
SWA_HQ = 16
SWA_HKV = 4
SWA_G = SWA_HQ // SWA_HKV
SWA_Q = 1024
SWA_KV = 256
WINDOW = 128
ROPE_THETA = 10000.0
X_H = 4
X_HD = 256
X_W = 1024
N_BRANCH = 3
EPS = 1e-6

LANES = 128
MXU_N = 256

_IN_SIZES = (GLA_QK, GLA_QK, GLA_V, GLA_V, GLA_RANK, SWA_Q, SWA_KV, SWA_KV, SWA_Q, X_W, X_W,
             N_BRANCH * D_MODEL)
_IN_OFF = np.concatenate([[0], np.cumsum(_IN_SIZES)]).astype(np.int64)
(_R_GQ, _R_GK, _R_GV, _R_GZ, _R_GLR, _R_SQ, _R_SK, _R_SV, _R_SZ, _R_XQ, _R_XZ, _R_GATES) = (
    int(o) for o in _IN_OFF[:-1])

GLR_PAD = LANES
C_GQ = 0
C_GK = C_GQ + GLA_QK
C_GV = C_GK + GLA_QK
C_GZ = C_GV + GLA_V
C_GLR = C_GZ + GLA_V
C_SQ = C_GLR + GLR_PAD
C_SK = C_SQ + SWA_Q
C_SV = C_SK + SWA_KV
C_SZ = C_SV + SWA_KV
C_XQ = C_SZ + SWA_Q
C_XZ = C_XQ + X_W
C_GATES = C_XZ + X_W
C_TOTAL = C_GATES + N_BRANCH * D_MODEL

FILL_ORDER = ("gz", "sz", "g0", "xq", "xz", "g1", "g2")
FILL_PER_GLA_TILE = 4
FILL_SOFTMAX_PERIOD = 4
NORM_ROWS = 128
OUT_ROWS = 256


def _swa_q_cols(a):
    lead = a.shape[:-1]
    nl = len(lead)
    a = a.reshape(lead + (SWA_HKV, SWA_G, 2, SWA_HALF))
    a = jnp.transpose(a, tuple(range(nl)) + (nl + 2, nl + 1, nl, nl + 3))
    return a.reshape(lead + (SWA_Q,))


def _swa_k_cols(a):
    lead = a.shape[:-1]
    nl = len(lead)
    a = a.reshape(lead + (SWA_HKV, 2, SWA_HALF))
    a = jnp.transpose(a, tuple(range(nl)) + (nl + 1, nl, nl + 2))
    return a.reshape(lead + (SWA_KV,))


def _swa_o_cols(a):
    lead = a.shape[:-1]
    nl = len(lead)
    a = a.reshape(lead + (SWA_HKV, SWA_G, SWA_HD))
    a = jnp.transpose(a, tuple(range(nl)) + (nl + 1, nl, nl + 2))
    return a.reshape(lead + (SWA_Q,))


def _mid_columns(raw):
    lead = raw.shape[:-1]
    seg = lambda off, n: lax.slice_in_dim(raw, off - _R_GLR, off - _R_GLR + n, axis=raw.ndim - 1)
    mid = jnp.concatenate([
        seg(_R_GLR, GLA_RANK),
        jnp.zeros(lead + (GLR_PAD - GLA_RANK,), raw.dtype),
        _swa_q_cols(seg(_R_SQ, SWA_Q)),
        _swa_k_cols(seg(_R_SK, SWA_KV)),
        seg(_R_SV, SWA_KV),
        _swa_o_cols(seg(_R_SZ, SWA_Q)),
    ], axis=-1)
    assert mid.shape[-1] == C_XQ - C_GLR
    return mid


def _kernel_columns(a):
    seg = lambda off, n: lax.slice_in_dim(a, off, off + n, axis=a.ndim - 1)
    return seg(_R_GQ, C_GLR), _mid_columns(seg(_R_GLR, _R_XQ - _R_GLR)), seg(_R_XQ, C_TOTAL - C_XQ)


def _column_block_sources():
    run = lambda off, n: [[(off + i * LANES, LANES)] for i in range(n // LANES)]
    blocks = run(_R_GQ, C_GLR)
    blocks.append([(_R_GLR, GLA_RANK), (None, GLR_PAD - GLA_RANK)])
    for t in range(2):
        for c in range(SWA_G):
            blocks.append([(_R_SQ + (j * SWA_G + c) * SWA_HD + t * SWA_HALF, SWA_HALF)
                           for j in range(SWA_HKV)])
    for t in range(2):
        blocks.append([(_R_SK + j * SWA_HD + t * SWA_HALF, SWA_HALF) for j in range(SWA_HKV)])
    blocks += run(_R_SV, SWA_KV)
    for c in range(SWA_G):
        for jp in range(SWA_HKV // 2):
            blocks.append([(_R_SZ + (j * SWA_G + c) * SWA_HD, SWA_HD) for j in (2 * jp, 2 * jp + 1)])
    blocks += run(_R_XQ, C_TOTAL - C_XQ)
    assert len(blocks) * LANES == C_TOTAL
    return blocks


def _w_in_cast_kernel(wt_ref, out_ref):
    for d, runs in enumerate(_column_block_sources()):
        parts = [jnp.zeros((n, LANES), jnp.float32) if src is None else wt_ref[src:src + n, :]
                 for src, n in runs]
        blk = parts[0] if len(parts) == 1 else jnp.concatenate(parts, axis=0)
        out_ref[:, d * LANES:(d + 1) * LANES] = blk.T.astype(jnp.bfloat16)


def _w_in_cast(w_t):
    n_cols, D = w_t.shape
    return pl.pallas_call(
        _w_in_cast_kernel,
        grid=(D // LANES,),
        in_specs=[pl.BlockSpec((n_cols, LANES), lambda i: (0, i))],
        out_specs=pl.BlockSpec((LANES, C_TOTAL), lambda i: (i, 0)),
        out_shape=jax.ShapeDtypeStruct((D, C_TOTAL), jnp.bfloat16),
        compiler_params=pltpu.CompilerParams(dimension_semantics=("arbitrary",)),
        name="w_in_cast",
    )(w_t)


def _dot(a, b):
    return jnp.dot(a, b, preferred_element_type=jnp.float32)


def _dot_nt(a, b):
    return lax.dot_general(a, b, (((1,), (1,)), ((), ())), preferred_element_type=jnp.float32)


def _dot_tn(a, b):
    return lax.dot_general(a, b, (((0,), (0,)), ((), ())), preferred_element_type=jnp.float32)


def _sigmoid(x):
    return 1.0 / (1.0 + jnp.exp(-x))


def _silu(x):
    return x * _sigmoid(x)


def _log_sigmoid(x):
    return jnp.minimum(x, 0.0) - jnp.log(1.0 + jnp.exp(-jnp.abs(x)))


def _split_bf16(x):
    hi = x.astype(jnp.bfloat16)
    lo = (x - hi.astype(jnp.float32)).astype(jnp.bfloat16)
    return hi, lo


def _mem_kv_kernel(mem_ref, g_ref, w_ref, out_ref):
    m = mem_ref[0]
    y = m * lax.rsqrt(jnp.mean(m * m, axis=-1, keepdims=True) + EPS) * g_ref[...]
    out_ref[0] = _dot(y.astype(jnp.bfloat16), w_ref[...]).astype(jnp.bfloat16)


def _mem_kv(mem, g_mem, w_mem_kv_bf16):
    B, M, D = mem.shape
    N = w_mem_kv_bf16.shape[1]
    return pl.pallas_call(
        _mem_kv_kernel,
        grid=(B,),
        in_specs=[
            pl.BlockSpec((1, M, D), lambda b: (b, 0, 0)),
            pl.BlockSpec((1, D), lambda b: (0, 0)),
            pl.BlockSpec((D, N), lambda b: (0, 0)),
        ],
        out_specs=pl.BlockSpec((1, M, N), lambda b: (b, 0, 0)),
        out_shape=jax.ShapeDtypeStruct((B, M, N), jnp.bfloat16),
        compiler_params=pltpu.CompilerParams(dimension_semantics=("arbitrary",)),
        name="mem_kv",
    )(mem, g_mem.reshape(1, D), w_mem_kv_bf16)


def _layer_kernel(sinks_ref, x_ref, mkv_ref, cos_ref, sin_ref, w_in_ref, b_in_ref,
                  w_up_ref, b_gate_ref, g_mix_ref, g_gla_ref, w_br_gla_ref, w_br_swa_ref, w_br_mem_ref,
                  w_out_ref, g_final_ref, tri_ref, bias_ref,
                  out_ref,
                  st_ref, kprev_ref, vprev_ref, ya_ref, yb_ref, yc_ref,
                  *, seq_tile, apply_final_norm):
    T = seq_tile
    f32 = jnp.float32
    bf16 = jnp.bfloat16
    s_idx = pl.program_id(1)

    rd = s_idx % 2
    wr = 1 - rd

    @pl.when(s_idx == 0)
    def _():
        st_ref[0] = jnp.zeros(st_ref.shape[1:], st_ref.dtype)
        kprev_ref[0] = jnp.zeros(kprev_ref.shape[1:], kprev_ref.dtype)
        vprev_ref[0] = jnp.zeros(vprev_ref.shape[1:], vprev_ref.dtype)

    x = x_ref[0]
    hb_parts = []
    for i in range(T // NORM_ROWS):
        xi = x_ref[0, i * NORM_ROWS:(i + 1) * NORM_ROWS]
        hb_parts.append((xi * lax.rsqrt(jnp.mean(xi * xi, axis=-1, keepdims=True) + EPS)
                         * g_mix_ref[...]).astype(bf16))
    hb = jnp.concatenate(hb_parts, axis=0)

    def proj(lo, width):
        return _dot(hb, w_in_ref[:, lo:lo + width]) + b_in_ref[:, lo:lo + width]

    n_slice = D_MODEL // MXU_N
    late = {}
    late_specs = {"gz": (C_GZ, None), "sz": (C_SZ, _silu), "xq": (C_XQ, None), "xz": (C_XZ, _silu),
                  "g0": (C_GATES, _sigmoid), "g1": (C_GATES + D_MODEL, _sigmoid),
                  "g2": (C_GATES + 2 * D_MODEL, _sigmoid)}
    pending = [(name, i) for name in FILL_ORDER for i in range(n_slice)]

    def fill(count):
        for _ in range(min(count, len(pending))):
            name, i = pending.pop(0)
            lo, act = late_specs[name]
            v = proj(lo + i * MXU_N, MXU_N)
            late[name, i] = v if act is None else act(v)

    def need(name):
        while any((name, i) not in late for i in range(n_slice)):
            fill(1)

    def late_full(name):
        need(name)
        return jnp.concatenate([late[name, i] for i in range(n_slice)], axis=1)

    glr = jnp.concatenate([_dot(hp, w_in_ref[:, C_GLR:C_GLR + GLR_PAD]) for hp in hb_parts],
                          axis=0) + b_in_ref[:, C_GLR:C_GLR + GLR_PAD]
    pre = _dot(glr.astype(bf16), w_up_ref[...]) + b_gate_ref[...]
    gq = proj(C_GQ, GLA_QK)
    gk = proj(C_GK, GLA_QK)
    gvb = proj(C_GV, GLA_V).astype(bf16)
    la = _log_sigmoid(pre) * (1.0 / GLA_TAU)
    la_hi, la_lo = _split_bf16(la)

    sq = proj(C_SQ, SWA_Q)
    sk = proj(C_SK, SWA_KV)
    vb = proj(C_SV, SWA_KV).astype(bf16)
    cs = cos_ref[...]
    sn = sin_ref[...]
    qscale = SWA_HD ** -0.5
    half = SWA_Q // 2
    q_cols = []
    for c in range(SWA_G):
        q1 = sq[:, c * LANES:(c + 1) * LANES]
        q2 = sq[:, half + c * LANES:half + (c + 1) * LANES]
        q_cols.append(jnp.concatenate([((q1 * cs - q2 * sn) * qscale).astype(bf16),
                                       ((q2 * cs + q1 * sn) * qscale).astype(bf16)], axis=1))
    k1 = sk[:, :LANES]
    k2 = sk[:, LANES:]
    kr = jnp.concatenate([k1 * cs - k2 * sn, k2 * cs + k1 * sn], axis=1).astype(bf16)
    kprev_ref[wr] = kr[T - WINDOW:T]
    vprev_ref[wr] = vb[T - WINDOW:T]

    tri = tri_ref[...]
    tri_mask = tri.astype(f32) > 0.5
    g_gla = g_gla_ref[...]
    n_chunk = GLA_TILE // GLA_CHUNK

    def chunk_columns(a):
        cols = []
        for c in range(n_chunk):
            parts = []
            if c > 0:
                parts.append(jnp.zeros((c * GLA_CHUNK, GLA_DK), a.dtype))
            parts.append(a[c * GLA_CHUNK:(c + 1) * GLA_CHUNK])
            if c < n_chunk - 1:
                parts.append(jnp.zeros(((n_chunk - 1 - c) * GLA_CHUNK, GLA_DK), a.dtype))
            cols.append(jnp.concatenate(parts, axis=0))
        return jnp.concatenate(cols, axis=1)

    state = [st_ref[rd, hh] for hh in range(GLA_H)]
    for g in range(T // GLA_TILE):
        r0 = g * GLA_TILE
        lah = la_hi[r0:r0 + GLA_TILE]
        lal = la_lo[r0:r0 + GLA_TILE]
        b = _dot(tri, lah) + _dot(tri, lal)
        bl = jnp.concatenate(
            [jnp.broadcast_to(b[(c + 1) * GLA_CHUNK - 1:(c + 1) * GLA_CHUNK], (GLA_CHUNK, GLA_QK))
             for c in range(n_chunk)], axis=0)
        gq_g = gq[r0:r0 + GLA_TILE]
        gk_g = gk[r0:r0 + GLA_TILE]
        qd = (gq_g * (GLA_DK ** -0.5) * jnp.exp(b)).astype(bf16)
        kn = (gk_g * jnp.exp(-b)).astype(bf16)
        kd = (gk_g * jnp.exp(bl - b)).astype(bf16)
        dec = jnp.exp(bl)
        fill(FILL_PER_GLA_TILE)
        need("gz")
        for hh in range(GLA_H):
            k0 = hh * GLA_DK
            v0 = hh * GLA_DV
            qd_h = qd[:, k0:k0 + GLA_DK]
            v_h = gvb[r0:r0 + GLA_TILE, v0:v0 + GLA_DV]
            att = jnp.where(tri_mask, _dot_nt(qd_h, kn[:, k0:k0 + GLA_DK]), 0.0).astype(bf16)
            o = _dot(att, v_h)
            upd = _dot_tn(v_h, chunk_columns(kd[:, k0:k0 + GLA_DK]))
            st = state[hh]
            sts = []
            for c in range(n_chunk):
                sts.append(st.astype(bf16))
                st = (dec[c * GLA_CHUNK:c * GLA_CHUNK + 1, k0:k0 + GLA_DK] * st
                      + upd[:, c * GLA_DK:(c + 1) * GLA_DK])
            state[hh] = st
            o = o + _dot_nt(chunk_columns(qd_h), jnp.concatenate(sts, axis=1))
            on = o * lax.rsqrt(jnp.mean(o * o, axis=-1, keepdims=True) + EPS) * g_gla
            ya_ref[r0:r0 + GLA_TILE, v0:v0 + GLA_DV] = (
                on * _silu(late["gz", hh][r0:r0 + GLA_TILE])).astype(bf16)
    for hh in range(GLA_H):
        st_ref[wr, hh] = state[hh]
    need("sz")
    merged = late_full("g0") * _dot(ya_ref[...], w_br_gla_ref[...])

    klane = lax.broadcasted_iota(jnp.int32, (1, SWA_KV), 1)
    kgrp = (klane % LANES) // SWA_HALF
    vgrp = klane // SWA_HD
    zero_b = jnp.zeros((), bf16)
    for n in range(T // WINDOW):
        r0 = n * WINDOW
        if n == 0:
            kp = kprev_ref[rd]
            vp = vprev_ref[rd]
            bias = bias_ref[jnp.where(s_idx == 0, 1, 0)]
        else:
            kp = kr[r0 - WINDOW:r0]
            vp = vb[r0 - WINDOW:r0]
            bias = bias_ref[0]
        kcat = jnp.concatenate([kp, kr[r0:r0 + WINDOW]], axis=0)
        vcat = jnp.concatenate([vp, vb[r0:r0 + WINDOW]], axis=0)
        rhs = jnp.concatenate([jnp.where(kgrp == j, kcat, zero_b) for j in range(SWA_HKV)], axis=0)
        vstk = jnp.concatenate([jnp.where(vgrp == j, vcat, zero_b) for j in range(SWA_HKV)], axis=0)
        lhs = jnp.concatenate([q_cols[c][r0:r0 + WINDOW] for c in range(SWA_G)], axis=0)
        s = _dot_nt(lhs, rhs)
        for c in range(SWA_G):
            ps = []
            for j in range(SWA_HKV):
                sc = s[c * WINDOW:(c + 1) * WINDOW, j * 2 * WINDOW:(j + 1) * 2 * WINDOW] + bias
                sink = sinks_ref[j * SWA_G + c]
                m = jnp.maximum(jnp.max(sc, axis=-1, keepdims=True), sink)
                p = jnp.exp(sc - m)
                l = jnp.sum(p, axis=-1, keepdims=True) + jnp.exp(sink - m)
                ps.append((p * (1.0 / l)).astype(bf16))
                if (c * SWA_HKV + j) % FILL_SOFTMAX_PERIOD == FILL_SOFTMAX_PERIOD - 1:
                    fill(1)
            o = _dot(jnp.concatenate(ps, axis=1), vstk)
            yb_ref[r0:r0 + WINDOW, c * SWA_KV:(c + 1) * SWA_KV] = (
                o * late["sz", c][r0:r0 + WINDOW]).astype(bf16)
    merged = merged + late_full("g1") * _dot(yb_ref[...], w_br_swa_ref[...])

    need("xq")
    need("xz")
    for hh in range(X_H):
        c0 = hh * X_HD
        qh = (late["xq", hh] * (X_HD ** -0.5)).astype(bf16)
        sc = _dot_nt(qh, mkv_ref[0, :, c0:c0 + X_HD])
        m = jnp.max(sc, axis=-1, keepdims=True)
        p = jnp.exp(sc - m)
        l = jnp.sum(p, axis=-1, keepdims=True)
        o = _dot((p * (1.0 / l)).astype(bf16), mkv_ref[0, :, X_W + c0:X_W + c0 + X_HD])
        yc_ref[:, c0:c0 + X_HD] = (o * late["xz", hh]).astype(bf16)
    merged = merged + late_full("g2") * _dot(yc_ref[...], w_br_mem_ref[...])

    mb = merged.astype(bf16)
    for i in range(T // OUT_ROWS):
        rows = slice(i * OUT_ROWS, (i + 1) * OUT_ROWS)
        xo = x_ref[0, rows] + _dot(mb[rows], w_out_ref[...])
        if apply_final_norm:
            xo = xo * lax.rsqrt(jnp.mean(xo * xo, axis=-1, keepdims=True) + EPS) * g_final_ref[...]
        out_ref[0, rows] = xo


def _seq_tile(S):
    for t in (512, 256, 128):
        if S % t == 0:
            return t
    raise ValueError(f"sequence length {S} must be a multiple of {WINDOW}")


def _layer(x, mkv, cos_t, sin_t, w_in, b_in, w_up, b_gate, g_mix, g_gla, sinks, w_br_gla, w_br_swa,
           w_br_mem, w_out, g_final, apply_final_norm):
    B, S, D = x.shape
    T = _seq_tile(S)
    M = mkv.shape[1]
    n_tiles = S // T

    rows = np.arange(GLA_TILE)
    same_chunk = (rows[:, None] // GLA_CHUNK) == (rows[None, :] // GLA_CHUNK)
    tri = jnp.asarray(same_chunk & (rows[None, :] <= rows[:, None]), jnp.bfloat16)
    qi = np.arange(WINDOW)[:, None]
    kj = np.arange(2 * WINDOW)[None, :]
    band = (kj <= qi + WINDOW) & (kj > qi)
    bias_np = np.stack([np.where(band, 0.0, -np.inf),
                        np.where(band & (kj >= WINDOW), 0.0, -np.inf)]).astype(np.float32)
    bias = jnp.asarray(bias_np)

    def const(shape):
        nd = len(shape)
        return pl.BlockSpec(shape, lambda b, s: (0,) * nd, pipeline_mode=pl.Buffered(1))

    in_specs = [
        pl.BlockSpec(memory_space=pltpu.SMEM),
        pl.BlockSpec((1, T, D), lambda b, s: (b, s, 0)),
        pl.BlockSpec((1, M, 2 * X_W), lambda b, s: (b, 0, 0)),
        pl.BlockSpec((T, LANES), lambda b, s: (s, 0)),
        pl.BlockSpec((T, LANES), lambda b, s: (s, 0)),
        const((D, C_TOTAL)),
        const((1, C_TOTAL)),
        const((GLR_PAD, GLA_QK)),
        const((1, GLA_QK)),
        const((1, D)),
        const((1, GLA_DV)),
        const((GLA_V, D)),
        const((SWA_Q, D)),
        const((X_W, D)),
        const((D, D)),
        const((1, D)),
        const((GLA_TILE, GLA_TILE)),
        const((2, WINDOW, 2 * WINDOW)),
    ]
    scratch = [
        pltpu.VMEM((2, GLA_H, GLA_DV, GLA_DK), jnp.float32),
        pltpu.VMEM((2, WINDOW, SWA_KV), jnp.bfloat16),
        pltpu.VMEM((2, WINDOW, SWA_KV), jnp.bfloat16),
        pltpu.VMEM((T, GLA_V), jnp.bfloat16),
        pltpu.VMEM((T, SWA_Q), jnp.bfloat16),
        pltpu.VMEM((T, X_W), jnp.bfloat16),
    ]
    kern = functools.partial(_layer_kernel, seq_tile=T, apply_final_norm=apply_final_norm)
    return pl.pallas_call(
        kern,
        grid=(B, n_tiles),
        in_specs=in_specs,
        out_specs=pl.BlockSpec((1, T, D), lambda b, s: (b, s, 0)),
        out_shape=jax.ShapeDtypeStruct((B, S, D), jnp.float32),
        scratch_shapes=scratch,
        compiler_params=pltpu.CompilerParams(
            dimension_semantics=("arbitrary", "arbitrary"),
            vmem_limit_bytes=60 * 1024 * 1024),
        name="layer",
    )(sinks, x, mkv, cos_t, sin_t, w_in, b_in, w_up, b_gate, g_mix, g_gla, w_br_gla, w_br_swa,
      w_br_mem, w_out, g_final, tri, bias)


def kernel(x, mem, g_mix, g_mem, w_in, b_in, w_gla_gate_up, b_gla_gate, g_gla_norm, sinks, w_mem_kv,
           w_br_gla, w_br_swa, w_br_mem, w_out, g_final):
    B, S, D = x.shape
    depth = w_in.shape[0]
    bf16 = jnp.bfloat16

    inv = ROPE_THETA ** (-jnp.arange(SWA_HALF, dtype=jnp.float32) / SWA_HALF)
    ang = jnp.arange(S).astype(jnp.float32)[:, None] * inv[None, :]
    reps = LANES // SWA_HALF
    cos_t = jnp.tile(jnp.cos(ang), (1, reps))
    sin_t = jnp.tile(jnp.sin(ang), (1, reps))

    for l in range(depth):
        w_in_k = _w_in_cast(jnp.transpose(w_in[l]))
        b_in_k = jnp.concatenate(_kernel_columns(b_in[l])).reshape(1, C_TOTAL)
        w_up = jnp.zeros((GLR_PAD, GLA_QK), bf16).at[:GLA_RANK].set(w_gla_gate_up[l].astype(bf16))
        w_br_swa_k = jnp.transpose(
            w_br_swa[l].astype(bf16).reshape(SWA_HKV, SWA_G, SWA_HD, D), (1, 0, 2, 3)).reshape(SWA_Q, D)
        mkv = _mem_kv(mem, g_mem[l], w_mem_kv[l].astype(bf16))
        x = _layer(
            x, mkv, cos_t, sin_t, w_in_k, b_in_k, w_up, b_gla_gate[l].reshape(1, GLA_QK),
            g_mix[l].reshape(1, D), g_gla_norm[l].reshape(1, GLA_DV), sinks[l],
            w_br_gla[l].astype(bf16), w_br_swa_k,
            w_br_mem[l].astype(bf16), w_out[l].astype(bf16), g_final.reshape(1, D),
            apply_final_norm=(l == depth - 1))
    return x
```

```python
import functools

import numpy as np
import jax
import jax.numpy as jnp
from jax import lax
from jax.experimental import pallas as pl
from jax.experimental.pallas import tpu as pltpu

D_MODEL = 1024
MEM_LEN = 256
GLA_H = 4
GLA_QK = 512
GLA_V = 1024
GLA_DK = 128
GLA_DV = 256
GLA_RANK = 16
GLA_TAU = 16.0
GLA_CHUNK = 64
GLA_TILE = 256
SWA_HD = 64
SWA_HALF = SWA_HD // 2
SWA_HQ = 16
SWA_HKV = 4
SWA_G = SWA_HQ // SWA_HKV
SWA_Q = 1024
SWA_KV = 256
WINDOW = 128
ROPE_THETA = 10000.0
X_H = 4
X_HD = 256
X_W = 1024
N_BRANCH = 3
EPS = 1e-6

LANES = 128
MXU_N = 256

_IN_SIZES = (GLA_QK, GLA_QK, GLA_V, GLA_V, GLA_RANK, SWA_Q, SWA_KV, SWA_KV, SWA_Q, X_W, X_W,
             N_BRANCH * D_MODEL)
_IN_OFF = np.concatenate([[0], np.cumsum(_IN_SIZES)]).astype(np.int64)
(_R_GQ, _R_GK, _R_GV, _R_GZ, _R_GLR, _R_SQ, _R_SK, _R_SV, _R_SZ, _R_XQ, _R_XZ, _R_GATES) = (
    int(o) for o in _IN_OFF[:-1])

GLR_PAD = LANES
C_GQ = 0
C_GK = C_GQ + GLA_QK
C_GV = C_GK + GLA_QK
C_GZ = C_GV + GLA_V
C_GLR = C_GZ + GLA_V
C_SQ = C_GLR + GLR_PAD
C_SK = C_SQ + SWA_Q
C_SV = C_SK + SWA_KV
C_SZ = C_SV + SWA_KV
C_XQ = C_SZ + SWA_Q
C_XZ = C_XQ + X_W
C_GATES = C_XZ + X_W
C_TOTAL = C_GATES + N_BRANCH * D_MODEL

FILL_ORDER = ("gz", "sz", "g0", "xq", "xz", "g1", "g2")
FILL_PER_GLA_TILE = 4
FILL_SOFTMAX_PERIOD = 4
NORM_ROWS = 128
OUT_ROWS = 256


def _swa_q_cols(a):
    lead = a.shape[:-1]
    nl = len(lead)
    a = a.reshape(lead + (SWA_HKV, SWA_G, 2, SWA_HALF))
    a = jnp.transpose(a, tuple(range(nl)) + (nl + 2, nl + 1, nl, nl + 3))
    return a.reshape(lead + (SWA_Q,))


def _swa_k_cols(a):
    lead = a.shape[:-1]
    nl = len(lead)
    a = a.reshape(lead + (SWA_HKV, 2, SWA_HALF))
    a = jnp.transpose(a, tuple(range(nl)) + (nl + 1, nl, nl + 2))
    return a.reshape(lead + (SWA_KV,))


def _swa_o_cols(a):
    lead = a.shape[:-1]
    nl = len(lead)
    a = a.reshape(lead + (SWA_HKV, SWA_G, SWA_HD))
    a = jnp.transpose(a, tuple(range(nl)) + (nl + 1, nl, nl + 2))
    return a.reshape(lead + (SWA_Q,))


def _mid_columns(raw):
    lead = raw.shape[:-1]
    seg = lambda off, n: lax.slice_in_dim(raw, off - _R_GLR, off - _R_GLR + n, axis=raw.ndim - 1)
    mid = jnp.concatenate([
        seg(_R_GLR, GLA_RANK),
        jnp.zeros(lead + (GLR_PAD - GLA_RANK,), raw.dtype),
        _swa_q_cols(seg(_R_SQ, SWA_Q)),
        _swa_k_cols(seg(_R_SK, SWA_KV)),
        seg(_R_SV, SWA_KV),
        _swa_o_cols(seg(_R_SZ, SWA_Q)),
    ], axis=-1)
    assert mid.shape[-1] == C_XQ - C_GLR
    return mid


def _kernel_columns(a):
    seg = lambda off, n: lax.slice_in_dim(a, off, off + n, axis=a.ndim - 1)
    return seg(_R_GQ, C_GLR), _mid_columns(seg(_R_GLR, _R_XQ - _R_GLR)), seg(_R_XQ, C_TOTAL - C_XQ)


def _column_block_sources():
    run = lambda off, n: [[(off + i * LANES, LANES)] for i in range(n // LANES)]
    blocks = run(_R_GQ, C_GLR)
    blocks.append([(_R_GLR, GLA_RANK), (None, GLR_PAD - GLA_RANK)])
    for t in range(2):
        for c in range(SWA_G):
            blocks.append([(_R_SQ + (j * SWA_G + c) * SWA_HD + t * SWA_HALF, SWA_HALF)
                           for j in range(SWA_HKV)])
    for t in range(2):
        blocks.append([(_R_SK + j * SWA_HD + t * SWA_HALF, SWA_HALF) for j in range(SWA_HKV)])
    blocks += run(_R_SV, SWA_KV)
    for c in range(SWA_G):
        for jp in range(SWA_HKV // 2):
            blocks.append([(_R_SZ + (j * SWA_G + c) * SWA_HD, SWA_HD) for j in (2 * jp, 2 * jp + 1)])
    blocks += run(_R_XQ, C_TOTAL - C_XQ)
    assert len(blocks) * LANES == C_TOTAL
    return blocks


def _w_in_cast_kernel(wt_ref, out_ref):
    for d, runs in enumerate(_column_block_sources()):
        parts = [jnp.zeros((n, LANES), jnp.float32) if src is None else wt_ref[src:src + n, :]
                 for src, n in runs]
        blk = parts[0] if len(parts) == 1 else jnp.concatenate(parts, axis=0)
        out_ref[:, d * LANES:(d + 1) * LANES] = blk.T.astype(jnp.bfloat16)


def _w_in_cast(w_t):
    n_cols, D = w_t.shape
    return pl.pallas_call(
        _w_in_cast_kernel,
        grid=(D // LANES,),
        in_specs=[pl.BlockSpec((n_cols, LANES), lambda i: (0, i))],
        out_specs=pl.BlockSpec((LANES, C_TOTAL), lambda i: (i, 0)),
        out_shape=jax.ShapeDtypeStruct((D, C_TOTAL), jnp.bfloat16),
        compiler_params=pltpu.CompilerParams(dimension_semantics=("arbitrary",)),
        name="w_in_cast",
    )(w_t)


def _dot(a, b):
    return jnp.dot(a, b, preferred_element_type=jnp.float32)


def _dot_nt(a, b):
    return lax.dot_general(a, b, (((1,), (1,)), ((), ())), preferred_element_type=jnp.float32)


def _dot_tn(a, b):
    return lax.dot_general(a, b, (((0,), (0,)), ((), ())), preferred_element_type=jnp.float32)


def _sigmoid(x):
    return 1.0 / (1.0 + jnp.exp(-x))


def _silu(x):
    return x * _sigmoid(x)


def _log_sigmoid(x):
    return jnp.minimum(x, 0.0) - jnp.log(1.0 + jnp.exp(-jnp.abs(x)))


def _split_bf16(x):
    hi = x.astype(jnp.bfloat16)
    lo = (x - hi.astype(jnp.float32)).astype(jnp.bfloat16)
    return hi, lo


def _mem_kv_kernel(mem_ref, g_ref, w_ref, out_ref, wb_ref):
    @pl.when(pl.program_id(0) == 0)
    def _():
        wb_ref[...] = w_ref[0].astype(jnp.bfloat16)

    nb, M, D = mem_ref.shape
    m = mem_ref[...].reshape(nb * M, D)
    y = m * lax.rsqrt(jnp.mean(m * m, axis=-1, keepdims=True) + EPS) * g_ref[...]
    kv = _dot(y.astype(jnp.bfloat16), wb_ref[...]).astype(jnp.bfloat16)
    out_ref[...] = kv.reshape(nb, M, kv.shape[-1])


def _mem_kv(mem, g_mem, w_mem_kv, layer):
    B, M, D = mem.shape
    N = w_mem_kv.shape[-1]
    nb = 2 if B % 2 == 0 else 1
    return pl.pallas_call(
        _mem_kv_kernel,
        grid=(B // nb,),
        in_specs=[
            pl.BlockSpec((nb, M, D), lambda b: (b, 0, 0)),
            pl.BlockSpec((1, D), lambda b: (0, 0)),
            pl.BlockSpec((1, D, N), lambda b: (layer, 0, 0), pipeline_mode=pl.Buffered(1)),
        ],
        out_specs=pl.BlockSpec((nb, M, N), lambda b: (b, 0, 0)),
        out_shape=jax.ShapeDtypeStruct((B, M, N), jnp.bfloat16),
        scratch_shapes=[pltpu.VMEM((D, N), jnp.bfloat16)],
        compiler_params=pltpu.CompilerParams(dimension_semantics=("arbitrary",)),
        name="mem_kv",
    )(mem, g_mem.reshape(1, D), w_mem_kv)


def _w_sq_cast_kernel(a_ref, s_ref, m_ref, o_ref, a_out, s_out, m_out, o_out):
    bf16 = jnp.bfloat16
    a_out[...] = a_ref[0].astype(bf16)
    m_out[...] = m_ref[0].astype(bf16)
    o_out[...] = o_ref[0].astype(bf16)
    for c in range(SWA_G):
        for j in range(SWA_HKV):
            src = (j * SWA_G + c) * SWA_HD
            dst = (c * SWA_HKV + j) * SWA_HD
            s_out[dst:dst + SWA_HD, :] = s_ref[0, src:src + SWA_HD, :].astype(bf16)


def _w_sq_cast(w_br_gla, w_br_swa, w_br_mem, w_out, layer):
    D = w_out.shape[-1]
    in_spec = pl.BlockSpec((1, D, MXU_N), lambda i: (layer, 0, i))
    out_spec = pl.BlockSpec((D, MXU_N), lambda i: (0, i))
    return pl.pallas_call(
        _w_sq_cast_kernel,
        grid=(D // MXU_N,),
        in_specs=[in_spec] * 4,
        out_specs=[out_spec] * 4,
        out_shape=[jax.ShapeDtypeStruct((D, D), jnp.bfloat16)] * 4,
        compiler_params=pltpu.CompilerParams(dimension_semantics=("arbitrary",)),
        name="w_sq_cast",
    )(w_br_gla, w_br_swa, w_br_mem, w_out)


def _layer_kernel(sinks_ref, x_ref, mkv_ref, cos_ref, sin_ref, w_in_ref, b_in_ref,
                  w_up_ref, b_gate_ref, g_mix_ref, g_gla_ref, w_br_gla_ref, w_br_swa_ref, w_br_mem_ref,
                  w_out_ref, g_final_ref, tri_ref, bias_ref,
                  out_ref,
                  st_ref, kprev_ref, vprev_ref, ya_ref, yb_ref, yc_ref,
                  *, seq_tile, apply_final_norm):
    T = seq_tile
    f32 = jnp.float32
    bf16 = jnp.bfloat16
    s_idx = pl.program_id(1)

    rd = s_idx % 2
    wr = 1 - rd

    @pl.when(s_idx == 0)
    def _():
        st_ref[0] = jnp.zeros(st_ref.shape[1:], st_ref.dtype)
        kprev_ref[0] = jnp.zeros(kprev_ref.shape[1:], kprev_ref.dtype)
        vprev_ref[0] = jnp.zeros(vprev_ref.shape[1:], vprev_ref.dtype)

    x = x_ref[0]
    hb_parts = []
    for i in range(T // NORM_ROWS):
        xi = x_ref[0, i * NORM_ROWS:(i + 1) * NORM_ROWS]
        hb_parts.append((xi * lax.rsqrt(jnp.mean(xi * xi, axis=-1, keepdims=True) + EPS)
                         * g_mix_ref[...]).astype(bf16))
    hb = jnp.concatenate(hb_parts, axis=0)

    def proj(lo, width):
        return _dot(hb, w_in_ref[:, lo:lo + width]) + b_in_ref[:, lo:lo + width]

    n_slice = D_MODEL // MXU_N
    late = {}
    late_specs = {"gz": (C_GZ, None), "sz": (C_SZ, _silu), "xq": (C_XQ, None), "xz": (C_XZ, _silu),
                  "g0": (C_GATES, _sigmoid), "g1": (C_GATES + D_MODEL, _sigmoid),
                  "g2": (C_GATES + 2 * D_MODEL, _sigmoid)}
    pending = [(name, i) for name in FILL_ORDER for i in range(n_slice)]

    def fill(count):
        for _ in range(min(count, len(pending))):
            name, i = pending.pop(0)
            lo, act = late_specs[name]
            v = proj(lo + i * MXU_N, MXU_N)
            late[name, i] = v if act is None else act(v)

    def need(name):
        while any((name, i) not in late for i in range(n_slice)):
            fill(1)

    def late_full(name):
        need(name)
        return jnp.concatenate([late[name, i] for i in range(n_slice)], axis=1)

    glr = jnp.concatenate([_dot(hp, w_in_ref[:, C_GLR:C_GLR + GLR_PAD]) for hp in hb_parts],
                          axis=0) + b_in_ref[:, C_GLR:C_GLR + GLR_PAD]
    pre = _dot(glr.astype(bf16), w_up_ref[...]) + b_gate_ref[...]
    gq = proj(C_GQ, GLA_QK)
    gk = proj(C_GK, GLA_QK)
    gvb = proj(C_GV, GLA_V).astype(bf16)
    la = _log_sigmoid(pre) * (1.0 / GLA_TAU)
    la_hi, la_lo = _split_bf16(la)

    sq = proj(C_SQ, SWA_Q)
    sk = proj(C_SK, SWA_KV)
    vb = proj(C_SV, SWA_KV).astype(bf16)
    cs = cos_ref[...]
    sn = sin_ref[...]
    qscale = SWA_HD ** -0.5
    half = SWA_Q // 2
    q_cols = []
    for c in range(SWA_G):
        q1 = sq[:, c * LANES:(c + 1) * LANES]
        q2 = sq[:, half + c * LANES:half + (c + 1) * LANES]
        q_cols.append(jnp.concatenate([((q1 * cs - q2 * sn) * qscale).astype(bf16),
                                       ((q2 * cs + q1 * sn) * qscale).astype(bf16)], axis=1))
    k1 = sk[:, :LANES]
    k2 = sk[:, LANES:]
    kr = jnp.concatenate([k1 * cs - k2 * sn, k2 * cs + k1 * sn], axis=1).astype(bf16)
    kprev_ref[wr] = kr[T - WINDOW:T]
    vprev_ref[wr] = vb[T - WINDOW:T]

    tri = tri_ref[...]
    tri_mask = tri.astype(f32) > 0.5
    g_gla = g_gla_ref[...]
    n_chunk = GLA_TILE // GLA_CHUNK

    def chunk_columns(a):
        cols = []
        for c in range(n_chunk):
            parts = []
            if c > 0:
                parts.append(jnp.zeros((c * GLA_CHUNK, GLA_DK), a.dtype))
            parts.append(a[c * GLA_CHUNK:(c + 1) * GLA_CHUNK])
            if c < n_chunk - 1:
                parts.append(jnp.zeros(((n_chunk - 1 - c) * GLA_CHUNK, GLA_DK), a.dtype))
            cols.append(jnp.concatenate(parts, axis=0))
        return jnp.concatenate(cols, axis=1)

    state = [st_ref[rd, hh] for hh in range(GLA_H)]
    for g in range(T // GLA_TILE):
        r0 = g * GLA_TILE
        lah = la_hi[r0:r0 + GLA_TILE]
        lal = la_lo[r0:r0 + GLA_TILE]
        b = _dot(tri, lah) + _dot(tri, lal)
        bl = jnp.concatenate(
            [jnp.broadcast_to(b[(c + 1) * GLA_CHUNK - 1:(c + 1) * GLA_CHUNK], (GLA_CHUNK, GLA_QK))
             for c in range(n_chunk)], axis=0)
        gq_g = gq[r0:r0 + GLA_TILE]
        gk_g = gk[r0:r0 + GLA_TILE]
        qd = (gq_g * (GLA_DK ** -0.5) * jnp.exp(b)).astype(bf16)
        kn = (gk_g * jnp.exp(-b)).astype(bf16)
        kd = (gk_g * jnp.exp(bl - b)).astype(bf16)
        dec = jnp.exp(bl)
        fill(FILL_PER_GLA_TILE)
        need("gz")
        for hh in range(GLA_H):
            k0 = hh * GLA_DK
            v0 = hh * GLA_DV
            qd_h = qd[:, k0:k0 + GLA_DK]
            v_h = gvb[r0:r0 + GLA_TILE, v0:v0 + GLA_DV]
            att = jnp.where(tri_mask, _dot_nt(qd_h, kn[:, k0:k0 + GLA_DK]), 0.0).astype(bf16)
            o = _dot(att, v_h)
            upd = _dot_tn(v_h, chunk_columns(kd[:, k0:k0 + GLA_DK]))
            st = state[hh]
            sts = []
            for c in range(n_chunk):
                sts.append(st.astype(bf16))
                st = (dec[c * GLA_CHUNK:c * GLA_CHUNK + 1, k0:k0 + GLA_DK] * st
                      + upd[:, c * GLA_DK:(c + 1) * GLA_DK])
            state[hh] = st
            o = o + _dot_nt(chunk_columns(qd_h), jnp.concatenate(sts, axis=1))
            on = o * lax.rsqrt(jnp.mean(o * o, axis=-1, keepdims=True) + EPS) * g_gla
            ya_ref[r0:r0 + GLA_TILE, v0:v0 + GLA_DV] = (
                on * _silu(late["gz", hh][r0:r0 + GLA_TILE])).astype(bf16)
    for hh in range(GLA_H):
        st_ref[wr, hh] = state[hh]
    need("sz")
    merged = late_full("g0") * _dot(ya_ref[...], w_br_gla_ref[...])

    klane = lax.broadcasted_iota(jnp.int32, (1, SWA_KV), 1)
    kgrp = (klane % LANES) // SWA_HALF
    vgrp = klane // SWA_HD
    zero_b = jnp.zeros((), bf16)
    for n in range(T // WINDOW):
        r0 = n * WINDOW
        if n == 0:
            kp = kprev_ref[rd]
            vp = vprev_ref[rd]
            bias = bias_ref[jnp.where(s_idx == 0, 1, 0)]
        else:
            kp = kr[r0 - WINDOW:r0]
            vp = vb[r0 - WINDOW:r0]
            bias = bias_ref[0]
        kcat = jnp.concatenate([kp, kr[r0:r0 + WINDOW]], axis=0)
        vcat = jnp.concatenate([vp, vb[r0:r0 + WINDOW]], axis=0)
        rhs = jnp.concatenate([jnp.where(kgrp == j, kcat, zero_b) for j in range(SWA_HKV)], axis=0)
        vstk = jnp.concatenate([jnp.where(vgrp == j, vcat, zero_b) for j in range(SWA_HKV)], axis=0)
        lhs = jnp.concatenate([q_cols[c][r0:r0 + WINDOW] for c in range(SWA_G)], axis=0)
        s = _dot_nt(lhs, rhs)
        for c in range(SWA_G):
            ps = []
            for j in range(SWA_HKV):
                sc = s[c * WINDOW:(c + 1) * WINDOW, j * 2 * WINDOW:(j + 1) * 2 * WINDOW] + bias
                sink = sinks_ref[j * SWA_G + c]
                m = jnp.maximum(jnp.max(sc, axis=-1, keepdims=True), sink)
                p = jnp.exp(sc - m)
                l = jnp.sum(p, axis=-1, keepdims=True) + jnp.exp(sink - m)
                ps.append((p * (1.0 / l)).astype(bf16))
                if (c * SWA_HKV + j) % FILL_SOFTMAX_PERIOD == FILL_SOFTMAX_PERIOD - 1:
                    fill(1)
            o = _dot(jnp.concatenate(ps, axis=1), vstk)
            yb_ref[r0:r0 + WINDOW, c * SWA_KV:(c + 1) * SWA_KV] = (
                o * late["sz", c][r0:r0 + WINDOW]).astype(bf16)
    merged = merged + late_full("g1") * _dot(yb_ref[...], w_br_swa_ref[...])

    need("xq")
    need("xz")
    for hh in range(X_H):
        c0 = hh * X_HD
        qh = (late["xq", hh] * (X_HD ** -0.5)).astype(bf16)
        sc = _dot_nt(qh, mkv_ref[0, :, c0:c0 + X_HD])
        m = jnp.max(sc, axis=-1, keepdims=True)
        p = jnp.exp(sc - m)
        l = jnp.sum(p, axis=-1, keepdims=True)
        o = _dot((p * (1.0 / l)).astype(bf16), mkv_ref[0, :, X_W + c0:X_W + c0 + X_HD])
        yc_ref[:, c0:c0 + X_HD] = (o * late["xz", hh]).astype(bf16)
    merged = merged + late_full("g2") * _dot(yc_ref[...], w_br_mem_ref[...])

    mb = merged.astype(bf16)
    for i in range(T // OUT_ROWS):
        rows = slice(i * OUT_ROWS, (i + 1) * OUT_ROWS)
        xo = x_ref[0, rows] + _dot(mb[rows], w_out_ref[...])
        if apply_final_norm:
            xo = xo * lax.rsqrt(jnp.mean(xo * xo, axis=-1, keepdims=True) + EPS) * g_final_ref[...]
        out_ref[0, rows] = xo


def _seq_tile(S):
    for t in (2 * GLA_TILE, GLA_TILE):
        if S % t == 0:
            return t
    raise ValueError(f"sequence length {S} must be a multiple of {GLA_TILE}")


def _layer(x, mkv, cos_t, sin_t, w_in, b_in, w_up, b_gate, g_mix, g_gla, sinks, w_br_gla, w_br_swa,
           w_br_mem, w_out, g_final, apply_final_norm):
    B, S, D = x.shape
    T = _seq_tile(S)
    M = mkv.shape[1]
    n_tiles = S // T

    rows = np.arange(GLA_TILE)
    same_chunk = (rows[:, None] // GLA_CHUNK) == (rows[None, :] // GLA_CHUNK)
    tri = jnp.asarray(same_chunk & (rows[None, :] <= rows[:, None]), jnp.bfloat16)
    qi = np.arange(WINDOW)[:, None]
    kj = np.arange(2 * WINDOW)[None, :]
    band = (kj <= qi + WINDOW) & (kj > qi)
    bias_np = np.stack([np.where(band, 0.0, -np.inf),
                        np.where(band & (kj >= WINDOW), 0.0, -np.inf)]).astype(np.float32)
    bias = jnp.asarray(bias_np)

    def const(shape):
        nd = len(shape)
        return pl.BlockSpec(shape, lambda b, s: (0,) * nd, pipeline_mode=pl.Buffered(1))

    in_specs = [
        pl.BlockSpec(memory_space=pltpu.SMEM),
        pl.BlockSpec((1, T, D), lambda b, s: (b, s, 0)),
        pl.BlockSpec((1, M, 2 * X_W), lambda b, s: (b, 0, 0)),
        pl.BlockSpec((T, LANES), lambda b, s: (s, 0)),
        pl.BlockSpec((T, LANES), lambda b, s: (s, 0)),
        const((D, C_TOTAL)),
        const((1, C_TOTAL)),
        const((GLR_PAD, GLA_QK)),
        const((1, GLA_QK)),
        const((1, D)),
        const((1, GLA_DV)),
        const((GLA_V, D)),
        const((SWA_Q, D)),
        const((X_W, D)),
        const((D, D)),
        const((1, D)),
        const((GLA_TILE, GLA_TILE)),
        const((2, WINDOW, 2 * WINDOW)),
    ]
    scratch = [
        pltpu.VMEM((2, GLA_H, GLA_DV, GLA_DK), jnp.float32),
        pltpu.VMEM((2, WINDOW, SWA_KV), jnp.bfloat16),
        pltpu.VMEM((2, WINDOW, SWA_KV), jnp.bfloat16),
        pltpu.VMEM((T, GLA_V), jnp.bfloat16),
        pltpu.VMEM((T, SWA_Q), jnp.bfloat16),
        pltpu.VMEM((T, X_W), jnp.bfloat16),
    ]
    kern = functools.partial(_layer_kernel, seq_tile=T, apply_final_norm=apply_final_norm)
    return pl.pallas_call(
        kern,
        grid=(B, n_tiles),
        in_specs=in_specs,
        out_specs=pl.BlockSpec((1, T, D), lambda b, s: (b, s, 0)),
        out_shape=jax.ShapeDtypeStruct((B, S, D), jnp.float32),
        scratch_shapes=scratch,
        compiler_params=pltpu.CompilerParams(
            dimension_semantics=("arbitrary", "arbitrary"),
            vmem_limit_bytes=60 * 1024 * 1024),
        name="layer",
    )(sinks, x, mkv, cos_t, sin_t, w_in, b_in, w_up, b_gate, g_mix, g_gla, w_br_gla, w_br_swa,
      w_br_mem, w_out, g_final, tri, bias)


def kernel(x, mem, g_mix, g_mem, w_in, b_in, w_gla_gate_up, b_gla_gate, g_gla_norm, sinks, w_mem_kv,
           w_br_gla, w_br_swa, w_br_mem, w_out, g_final):
    B, S, D = x.shape
    depth = w_in.shape[0]
    bf16 = jnp.bfloat16

    inv = ROPE_THETA ** (-jnp.arange(SWA_HALF, dtype=jnp.float32) / SWA_HALF)
    ang = jnp.arange(S).astype(jnp.float32)[:, None] * inv[None, :]
    reps = LANES // SWA_HALF
    cos_t = jnp.tile(jnp.cos(ang), (1, reps))
    sin_t = jnp.tile(jnp.sin(ang), (1, reps))

    for l in range(depth):
        w_in_k = _w_in_cast(jnp.transpose(w_in[l]))
        b_in_k = jnp.concatenate(_kernel_columns(b_in[l])).reshape(1, C_TOTAL)
        w_up = jnp.zeros((GLR_PAD, GLA_QK), bf16).at[:GLA_RANK].set(w_gla_gate_up[l].astype(bf16))
        w_gla_k, w_swa_k, w_mem_k, w_out_k = _w_sq_cast(w_br_gla, w_br_swa, w_br_mem, w_out, l)
        mkv = _mem_kv(mem, g_mem[l], w_mem_kv, l)
        x = _layer(
            x, mkv, cos_t, sin_t, w_in_k, b_in_k, w_up, b_gla_gate[l].reshape(1, GLA_QK),
            g_mix[l].reshape(1, D), g_gla_norm[l].reshape(1, GLA_DV), sinks[l],
            w_gla_k, w_swa_k, w_mem_k, w_out_k, g_final.reshape(1, D),
            apply_final_norm=(l == depth - 1))
    return x
```
